```python
import math
import jax, jax.numpy as jnp
from jax import lax
import numpy as np

D_MODEL = 1024
BATCH = 2
SEQ = 16384
DEPTH = 4

CHUNK = 64
Q_BLOCK = 128
CONV_K = 4
EPS = 1e-6

GLA_HEADS = 4
GLA_DK = 64
GLA_DV = 128
GLA_QK = GLA_HEADS * GLA_DK
GLA_WIDTH = GLA_HEADS * GLA_DV
GLA_GATE_RANK = 16
GLA_GATE_TEMP = 16.0

DIFF_HEADS = 4
DIFF_DH = 64
DIFF_DV = 2 * DIFF_DH
DIFF_QK = DIFF_HEADS * 2 * DIFF_DH
DIFF_WIDTH = DIFF_HEADS * DIFF_DV

MIX_WIDTH = GLA_WIDTH + DIFF_WIDTH
GLA_CONV_WIDTH = 2 * GLA_QK + GLA_WIDTH
SPLIT_SIZES = (GLA_QK, GLA_QK, GLA_WIDTH, GLA_GATE_RANK, GLA_WIDTH,
               DIFF_QK, DIFF_QK, DIFF_WIDTH, DIFF_WIDTH)
IN_WIDTH = 2 * GLA_QK + 2 * GLA_WIDTH + GLA_GATE_RANK + 2 * DIFF_QK + 2 * DIFF_WIDTH

kernel_name = "hybrid_gla_diffattn_adaln_trunk"


def rms_norm(x, g):
    xf = x.astype(jnp.float32)
    y = xf * lax.rsqrt(jnp.mean(xf * xf, axis=-1, keepdims=True) + EPS)
    return (y * g.astype(jnp.float32)).astype(x.dtype)


def split_columns(p):
    idx = np.cumsum(np.array(SPLIT_SIZES))[:-1].tolist()
    return jnp.split(p, idx, axis=-1)


def causal_depthwise_conv(u, w):
    S = u.shape[1]
    up = jnp.pad(u, ((0, 0), (CONV_K - 1, 0), (0, 0)))
    out = up[:, 0:S] * w[0]
    for j in range(1, CONV_K):
        out = out + up[:, j:j + S] * w[j]
    return out


def gla_branch(q, k, v, glr, z, w_gk, b_gk, g_out):
    B, S, _ = q.shape
    nc = S // CHUNK
    f32 = jnp.float32
    q = q.astype(f32).reshape(B, nc, CHUNK, GLA_HEADS, GLA_DK) * (GLA_DK ** -0.5)
    k = k.astype(f32).reshape(B, nc, CHUNK, GLA_HEADS, GLA_DK)
    v = v.astype(f32).reshape(B, nc, CHUNK, GLA_HEADS, GLA_DV)
    log_a = jax.nn.log_sigmoid((glr @ w_gk + b_gk).astype(f32)) / GLA_GATE_TEMP
    log_a = log_a.reshape(B, nc, CHUNK, GLA_HEADS, GLA_DK)
    b = jnp.cumsum(log_a, axis=2)
    b_end = b[:, :, -1]
    k_dec = k * jnp.exp(b_end[:, :, None] - b)
    u = jnp.einsum('bnchk,bnchv->nbhkv', k_dec, v)
    a = jnp.exp(b_end).transpose(1, 0, 2, 3)

    def step(state, inp):
        a_c, u_c = inp
        state = a_c[..., None] * state + u_c
        return state, state

    s0 = jnp.zeros((B, GLA_HEADS, GLA_DK, GLA_DV), f32)
    _, states = lax.scan(step, s0, (a, u))
    o = jnp.einsum('bnchk,nbhkv->bnchv', q, states)
    o = rms_norm(o, g_out).reshape(B, S, GLA_WIDTH)
    return o * jax.nn.silu(z.astype(f32))


def diff_branch(q, k, v, z, qn_g, kn_g, lam, lam_init, g_out):
    B, S, _ = q.shape
    f32 = jnp.float32
    q = rms_norm(q.reshape(B, S, DIFF_HEADS, 2, DIFF_DH), qn_g) * (DIFF_DH ** -0.5)
    k = rms_norm(k.reshape(B, S, DIFF_HEADS, 2, DIFF_DH), kn_g)
    v = v.astype(f32).reshape(B, S, DIFF_HEADS, DIFF_DV)
    key_chunk = jnp.arange(S) // CHUNK
    nb = S // Q_BLOCK
    qb = q.reshape(B, nb, Q_BLOCK, DIFF_HEADS, 2, DIFF_DH).transpose(1, 0, 2, 3, 4, 5)
    q_chunk = key_chunk.reshape(nb, Q_BLOCK)

    def block(args):
        qi, qc = args
        s = jnp.einsum('bqhmd,bkhmd->bhmqk', qi, k).astype(f32)
        mask = key_chunk[None, :] <= qc[:, None]
        s = jnp.where(mask, s, -jnp.inf)
        p = jax.nn.softmax(s, axis=-1)
        attn = p[:, :, 0] - lam * p[:, :, 1]
        return jnp.einsum('bhqk,bkhe->bqhe', attn, v)

    o = lax.map(block, (qb, q_chunk))
    o = o.transpose(1, 0, 2, 3, 4).reshape(B, S, DIFF_HEADS, DIFF_DV)
    o = rms_norm(o, g_out) * (1.0 - lam_init)
    return o.reshape(B, S, DIFF_WIDTH) * jax.nn.silu(z.astype(f32))


def setup_inputs(seed: int = 0) -> dict:
    key = jax.random.key(seed)
    ks = jax.random.split(key, 18)
    f32 = jnp.float32
    L, D = DEPTH, D_MODEL
    n = lambda k, s, sc: (jax.random.normal(k, s, f32) * sc)
    return {
        "x": n(ks[0], (BATCH, SEQ, D), 1.0),
        "c": n(ks[1], (BATCH, D), 1.0),
        "w_ada": n(ks[2], (L, D, 3 * D), 0.5 * D ** -0.5),
        "b_ada": n(ks[3], (L, 3 * D), 0.02),
        "norm_g": 1.0 + n(ks[4], (L, D), 0.02),
        "w_in": n(ks[5], (L, D, IN_WIDTH), D ** -0.5),
        "conv_w": n(ks[6], (L, CONV_K, GLA_CONV_WIDTH), CONV_K ** -0.5),
        "w_gk": n(ks[7], (L, GLA_GATE_RANK, GLA_QK), GLA_GATE_RANK ** -0.5),
        "b_gk": n(ks[8], (L, GLA_QK), 0.1),
        "gla_norm_g": 1.0 + n(ks[9], (L, GLA_DV), 0.02),
        "qn_g": 1.0 + n(ks[10], (L, DIFF_DH), 0.02),
        "kn_g": 1.0 + n(ks[11], (L, DIFF_DH), 0.02),
        "lam_q1": n(ks[12], (L, DIFF_DH), 0.1),
        "lam_k1": n(ks[13], (L, DIFF_DH), 0.1),
        "lam_q2": n(ks[14], (L, DIFF_DH), 0.1),
        "lam_k2": n(ks[15], (L, DIFF_DH), 0.1),
        "diff_norm_g": 1.0 + n(ks[16], (L, DIFF_DV), 0.02),
        "w_out": n(ks[17], (L, MIX_WIDTH, D), MIX_WIDTH ** -0.5),
    }


def reference(x, c, w_ada, b_ada, norm_g, w_in, conv_w, w_gk, b_gk, gla_norm_g,
              qn_g, kn_g, lam_q1, lam_k1, lam_q2, lam_k2, diff_norm_g, w_out):
    f32 = jnp.float32
    c_act = jax.nn.silu(c)
    for l in range(DEPTH):
        mod = c_act @ w_ada[l] + b_ada[l]
        shift, scale, gate = jnp.split(mod, 3, axis=-1)
        h = rms_norm(x, norm_g[l]) * (1.0 + scale[:, None]) + shift[:, None]
        proj = h @ w_in[l]
        gq, gk, gv, glr, gz, dq, dk, dv, dz = split_columns(proj)
        gqkv = jax.nn.silu(causal_depthwise_conv(jnp.concatenate([gq, gk, gv], axis=-1), conv_w[l]))
        gq, gk, gv = jnp.split(gqkv, [GLA_QK, 2 * GLA_QK], axis=-1)
        o_gla = gla_branch(gq, gk, gv, glr, gz, w_gk[l], b_gk[l], gla_norm_g[l])
        lam_init = 0.8 - 0.6 * math.exp(-0.3 * l)
        lam = (jnp.exp(jnp.sum(lam_q1[l].astype(f32) * lam_k1[l].astype(f32)))
               - jnp.exp(jnp.sum(lam_q2[l].astype(f32) * lam_k2[l].astype(f32))) + lam_init)
        o_diff = diff_branch(dq, dk, dv, dz, qn_g[l], kn_g[l], lam, lam_init, diff_norm_g[l])
        y = jnp.concatenate([o_gla, o_diff], axis=-1).astype(x.dtype) @ w_out[l]
        x = x + gate[:, None] * y
    return x
```

```python
import functools
import math

import jax
import jax.numpy as jnp
from jax import lax
from jax.experimental import pallas as pl
from jax.experimental.pallas import tpu as pltpu

D_MODEL = 1024
DEPTH = 4
CHUNK = 64
CONV_K = 4
EPS = 1e-6

GLA_HEADS = 4
GLA_DK = 64
GLA_DV = 128
GLA_QK = GLA_HEADS * GLA_DK
GLA_WIDTH = GLA_HEADS * GLA_DV
GLA_GATE_RANK = 16
GLA_GATE_TEMP = 16.0

DIFF_HEADS = 4
DIFF_DH = 64
DIFF_DV = 2 * DIFF_DH
DIFF_QK = DIFF_HEADS * 2 * DIFF_DH
DIFF_WIDTH = DIFF_HEADS * DIFF_DV

MIX_WIDTH = GLA_WIDTH + DIFF_WIDTH
GLA_CONV_WIDTH = 2 * GLA_QK + GLA_WIDTH
SPLIT_SIZES = (GLA_QK, GLA_QK, GLA_WIDTH, GLA_GATE_RANK, GLA_WIDTH,
               DIFF_QK, DIFF_QK, DIFF_WIDTH, DIFF_WIDTH)

LANES = 128
GATE_PAD = LANES
VMEM_LIMIT = 56 * 1024 * 1024

PROJ_TM = 512
OUT_TM = 512
GLA_T = 512
ATT_TQ = 512
ATT_TK = 512
MOD_TN = 512

F32 = jnp.float32
BF16 = jnp.bfloat16


def _sigmoid(x):
    return 1.0 / (1.0 + jnp.exp(-x))


def _silu(x):
    return x * _sigmoid(x)


def _mod_kernel(ct_ref, w_ref, b_ref, o_ref):
    ct = ct_ref[...]
    ca = _silu(ct)
    w = w_ref[0]
    rows = []
    for b in range(ct.shape[1]):
        rows.append(jnp.sum(w * ca[:, b:b + 1], axis=0, keepdims=True))
    o_ref[0] = jnp.concatenate(rows, axis=0) + b_ref[0]


def _modulation(c, w_ada, b_ada):
    B, D = c.shape
    L, _, N = w_ada.shape
    return pl.pallas_call(
        _mod_kernel,
        out_shape=jax.ShapeDtypeStruct((L, B, N), F32),
        grid=(L, N // MOD_TN),
        in_specs=[
            pl.BlockSpec((D, B), lambda l, n: (0, 0)),
            pl.BlockSpec((1, D, MOD_TN), lambda l, n: (l, 0, n)),
            pl.BlockSpec((1, 1, MOD_TN), lambda l, n: (l, 0, n)),
        ],
        out_specs=pl.BlockSpec((1, B, MOD_TN), lambda l, n: (l, 0, n)),
        compiler_params=pltpu.CompilerParams(
            dimension_semantics=("arbitrary", "arbitrary"), vmem_limit_bytes=VMEM_LIMIT),
        name="adaln_mod",
    )(c.T, w_ada, b_ada.reshape(L, 1, N))


def _group_rms(t, ones_bd, gain):
    ss = jnp.dot((t * t).astype(BF16), ones_bd, preferred_element_type=F32)
    return t * lax.rsqrt(ss * (1.0 / DIFF_DH) + EPS) * gain


def _proj_kernel(x_ref, mod_ref, g_ref, w_ref, ones_ref, qg_ref, kg_ref,
                 gqkv_ref, gz_ref, dq_ref, dk_ref, dv_ref, dz_ref, glr_ref):
    D = D_MODEL
    x = x_ref[0]
    mod = mod_ref[0]
    shift = mod[:, 0:D]
    scale = mod[:, D:2 * D]
    ms = jnp.mean(x * x, axis=-1, keepdims=True)
    y = x * lax.rsqrt(ms + EPS) * g_ref[...]
    h = (y * (1.0 + scale) + shift).astype(BF16)

    def proj(a, n):
        return jnp.dot(h, w_ref[:, a:a + n], preferred_element_type=F32)

    gqkv_ref[0, :, 0:512] = proj(0, 512).astype(BF16)
    gqkv_ref[0, :, 512:1024] = proj(512, 512).astype(BF16)
    gz_ref[0] = proj(1024, 512).astype(BF16)
    ones_bd = ones_ref[...]
    dq = _group_rms(proj(1536, 512), ones_bd, qg_ref[...]) * (DIFF_DH ** -0.5)
    dq_ref[0] = dq.astype(BF16)
    dk_ref[0] = _group_rms(proj(2048, 512), ones_bd, kg_ref[...]).astype(BF16)
    dv_ref[0] = proj(2560, 512).astype(BF16)
    dz_ref[0] = proj(3072, 512).astype(BF16)
    glr_ref[0] = proj(3584, GATE_PAD).astype(BF16)


def _in_projection(x, mod_l, norm_g, w_perm, ones_bd, qg, kg):
    B, S, D = x.shape
    tm = PROJ_TM
    NW = w_perm.shape[1]
    row = lambda b, i: (b, i, 0)
    const = lambda b, i: (0, 0)
    widths = (1024, 512, 512, 512, 512, 512, GATE_PAD)
    return pl.pallas_call(
        _proj_kernel,
        out_shape=[jax.ShapeDtypeStruct((B, S, w), BF16) for w in widths],
        grid=(B, S // tm),
        in_specs=[
            pl.BlockSpec((1, tm, D), row),
            pl.BlockSpec((1, 1, 3 * D), lambda b, i: (b, 0, 0)),
            pl.BlockSpec((1, D), const),
            pl.BlockSpec((D, NW), const, pipeline_mode=pl.Buffered(1)),
            pl.BlockSpec((512, 512), const, pipeline_mode=pl.Buffered(1)),
            pl.BlockSpec((1, 512), const),
            pl.BlockSpec((1, 512), const),
        ],
        out_specs=[pl.BlockSpec((1, tm, w), row) for w in widths],
        compiler_params=pltpu.CompilerParams(
            dimension_semantics=("arbitrary", "arbitrary"), vmem_limit_bytes=VMEM_LIMIT),
        name="in_proj",
    )(x, mod_l, norm_g, w_perm, ones_bd, qg, kg)


def _gla_kernel(u_ref, glr_ref, z_ref, cw_ref, wgk_ref, bgk_ref, tri_ref, g_ref,
                o_ref, xbuf, state):
    T = GLA_T
    t_idx = pl.program_id(1)

    @pl.when(t_idx == 0)
    def _():
        xbuf[0:8, :] = jnp.zeros((8, GLA_CONV_WIDTH), F32)
        state[...] = jnp.zeros(state.shape, F32)

    xbuf[8:8 + T, :] = u_ref[0].astype(F32)
    cw = cw_ref[...]
    conv = cw[0:1, :] * xbuf[pl.ds(8 - (CONV_K - 1), T), :]
    for j in range(1, CONV_K):
        conv = conv + cw[j:j + 1, :] * xbuf[pl.ds(8 - (CONV_K - 1) + j, T), :]
    xbuf[0:8, :] = xbuf[T:T + 8, :]
    act = _silu(conv)
    q = (act[:, 0:GLA_QK] * (GLA_DK ** -0.5)).astype(BF16)
    k = act[:, GLA_QK:2 * GLA_QK]
    v = act[:, 2 * GLA_QK:].astype(BF16)

    gate = jnp.dot(glr_ref[0], wgk_ref[...], preferred_element_type=F32) + bgk_ref[...]
    log_a = -(jnp.maximum(-gate, 0.0) + jnp.log1p(jnp.exp(-jnp.abs(gate))))
    log_a = log_a * (1.0 / GLA_GATE_TEMP)
    la_hi = log_a.astype(BF16)
    la_lo = (log_a - la_hi.astype(F32)).astype(BF16)
    tri = tri_ref[...]
    bcum = (jnp.dot(tri, la_hi, preferred_element_type=F32)
            + jnp.dot(tri, la_lo, preferred_element_type=F32))

    lane = lax.broadcasted_iota(jnp.int32, (CHUNK, LANES), 1)
    g_out = g_ref[...]
    for c in range(T // CHUNK):
        r0 = c * CHUNK
        b_c = bcum[r0:r0 + CHUNK, :]
        b_end = b_c[CHUNK - 1:CHUNK, :]
        kdec = k[r0:r0 + CHUNK, :] * jnp.exp(b_end - b_c)
        a_c = jnp.exp(b_end)
        for p in range(GLA_HEADS // 2):
            kd_pair = kdec[:, p * LANES:(p + 1) * LANES]
            q_pair = q[r0:r0 + CHUNK, p * LANES:(p + 1) * LANES]
            a_pair = a_c[:, p * LANES:(p + 1) * LANES]
            for hh in range(2):
                h = 2 * p + hh
                in_head = (lane >= hh * GLA_DK) & (lane < (hh + 1) * GLA_DK)
                kd_m = jnp.where(in_head, kd_pair, 0.0).astype(BF16)
                v_h = v[r0:r0 + CHUNK, h * GLA_DV:(h + 1) * GLA_DV]
                u_t = lax.dot_general(v_h, kd_m, (((0,), (0,)), ((), ())),
                                      preferred_element_type=F32)
                s_new = state[h] * a_pair + u_t
                state[h] = s_new
                o_h = lax.dot_general(q_pair, s_new.astype(BF16), (((1,), (1,)), ((), ())),
                                      preferred_element_type=F32)
                ms = jnp.mean(o_h * o_h, axis=-1, keepdims=True)
                zz = z_ref[0, r0:r0 + CHUNK, h * GLA_DV:(h + 1) * GLA_DV].astype(F32)
                res = o_h * lax.rsqrt(ms + EPS) * g_out * _silu(zz)
                o_ref[0, r0:r0 + CHUNK, h * GLA_DV:(h + 1) * GLA_DV] = res.astype(BF16)


def _gla_branch(gqkv, glr, gz, conv_w8, wgk_pad, bgk, tri, g_out):
    B, S, _ = gqkv.shape
    T = GLA_T
    row = lambda b, t: (b, t, 0)
    const = lambda b, t: (0, 0)
    return pl.pallas_call(
        _gla_kernel,
        out_shape=jax.ShapeDtypeStruct((B, S, GLA_WIDTH), BF16),
        grid=(B, S // T),
        in_specs=[
            pl.BlockSpec((1, T, GLA_CONV_WIDTH), row),
            pl.BlockSpec((1, T, GATE_PAD), row),
            pl.BlockSpec((1, T, GLA_WIDTH), row),
            pl.BlockSpec((8, GLA_CONV_WIDTH), const),
            pl.BlockSpec((GATE_PAD, GLA_QK), const),
            pl.BlockSpec((1, GLA_QK), const),
            pl.BlockSpec((T, T), const),
            pl.BlockSpec((1, GLA_DV), const),
        ],
        out_specs=pl.BlockSpec((1, T, GLA_WIDTH), row),
        scratch_shapes=[
            pltpu.VMEM((T + 8, GLA_CONV_WIDTH), F32),
            pltpu.VMEM((GLA_HEADS, GLA_DV, LANES), F32),
        ],
        compiler_params=pltpu.CompilerParams(
            dimension_semantics=("arbitrary", "arbitrary"), vmem_limit_bytes=VMEM_LIMIT),
        name="gla_branch",
    )(gqkv, glr, gz, conv_w8, wgk_pad, bgk, tri, g_out)


def _attn_kernel(q_ref, k_ref, v_ref, z_ref, lq1_ref, lk1_ref, lq2_ref, lk2_ref,
                 li_ref, g_ref, o_ref, m_s, l_s, acc_s):
    tq, tk = ATT_TQ, ATT_TK
    i = pl.program_id(2)
    q = q_ref[0]
    lane = lax.broadcasted_iota(jnp.int32, (tq, LANES), 1)
    zero = jnp.zeros_like(q)
    q_maps = (jnp.where(lane < DIFF_DH, q, zero), jnp.where(lane >= DIFF_DH, q, zero))

    m_s[...] = jnp.full(m_s.shape, -jnp.inf, F32)
    l_s[...] = jnp.zeros(l_s.shape, F32)
    acc_s[...] = jnp.zeros(acc_s.shape, F32)

    def step(j, mask):
        kj = k_ref[0, pl.ds(pl.multiple_of(j * tk, tk), tk), :]
        vj = v_ref[0, pl.ds(pl.multiple_of(j * tk, tk), tk), :]
        for mp in range(2):
            s = lax.dot_general(q_maps[mp], kj, (((1,), (1,)), ((), ())),
                                preferred_element_type=F32)
            if mask is not None:
                s = jnp.where(mask, s, -jnp.inf)
            m_old = m_s[mp]
            m_new = jnp.maximum(m_old, jnp.max(s, axis=-1, keepdims=True))
            alpha = jnp.exp(m_old - m_new)
            p = jnp.exp(s - m_new)
            l_s[mp] = alpha * l_s[mp] + jnp.sum(p, axis=-1, keepdims=True)
            acc_s[mp] = alpha * acc_s[mp] + jnp.dot(p.astype(BF16), vj,
                                                    preferred_element_type=F32)
            m_s[mp] = m_new

    def body(j, carry):
        step(j, None)
        return carry

    lax.fori_loop(0, i, body, 0)
    rq = lax.broadcasted_iota(jnp.int32, (tq, tk), 0) // CHUNK
    ck = lax.broadcasted_iota(jnp.int32, (tq, tk), 1) // CHUNK
    step(i, ck <= rq)

    lam_init = li_ref[...]
    lam = (jnp.exp(jnp.sum(lq1_ref[...] * lk1_ref[...], axis=-1, keepdims=True))
           - jnp.exp(jnp.sum(lq2_ref[...] * lk2_ref[...], axis=-1, keepdims=True))
           + lam_init)
    out = acc_s[0] / l_s[0] - lam * (acc_s[1] / l_s[1])
    ms = jnp.mean(out * out, axis=-1, keepdims=True)
    out = out * lax.rsqrt(ms + EPS) * g_ref[...] * (1.0 - lam_init)
    o_ref[0] = (out * _silu(z_ref[0].astype(F32))).astype(BF16)


def _diff_attention(dq, dk, dv, dz, lq1, lk1, lq2, lk2, lam_init_row, g_out):
    B, S, _ = dq.shape
    tq = ATT_TQ
    H = DIFF_HEADS
    qmap = lambda b, h, i: (b, i, h)
    kvmap = lambda b, h, i: (b, 0, h)
    const = lambda b, h, i: (0, 0)
    return pl.pallas_call(
        _attn_kernel,
        out_shape=jax.ShapeDtypeStruct((B, S, DIFF_WIDTH), BF16),
        grid=(B, H, S // tq),
        in_specs=[
            pl.BlockSpec((1, tq, LANES), qmap),
            pl.BlockSpec((1, S, LANES), kvmap),
            pl.BlockSpec((1, S, LANES), kvmap),
            pl.BlockSpec((1, tq, LANES), qmap),
            pl.BlockSpec((1, DIFF_DH), const),
            pl.BlockSpec((1, DIFF_DH), const),
            pl.BlockSpec((1, DIFF_DH), const),
            pl.BlockSpec((1, DIFF_DH), const),
            pl.BlockSpec((1, LANES), const),
            pl.BlockSpec((1, DIFF_DV), const),
        ],
        out_specs=pl.BlockSpec((1, tq, LANES), qmap),
        scratch_shapes=[
            pltpu.VMEM((2, tq, 1), F32),
            pltpu.VMEM((2, tq, 1), F32),
            pltpu.VMEM((2, tq, DIFF_DV), F32),
        ],
        compiler_params=pltpu.CompilerParams(
            dimension_semantics=("arbitrary", "arbitrary", "arbitrary"),
            vmem_limit_bytes=VMEM_LIMIT),
        name="diff_attn",
    )(dq, dk, dv, dz, lq1, lk1, lq2, lk2, lam_init_row, g_out)


def _out_kernel(og_ref, od_ref, x_ref, mod_ref, w_ref, o_ref):
    D = D_MODEL
    gate = mod_ref[0][:, 2 * D:3 * D]
    y = (jnp.dot(og_ref[0], w_ref[0:GLA_WIDTH, :], preferred_element_type=F32)
         + jnp.dot(od_ref[0], w_ref[GLA_WIDTH:MIX_WIDTH, :], preferred_element_type=F32))
    o_ref[0] = x_ref[0] + gate * y


def _out_projection(o_gla, o_diff, x, mod_l, w_out):
    B, S, D = x.shape
    tm = OUT_TM
    row = lambda b, i: (b, i, 0)
    return pl.pallas_call(
        _out_kernel,
        out_shape=jax.ShapeDtypeStruct((B, S, D), F32),
        grid=(B, S // tm),
        in_specs=[
            pl.BlockSpec((1, tm, GLA_WIDTH), row),
            pl.BlockSpec((1, tm, DIFF_WIDTH), row),
            pl.BlockSpec((1, tm, D), row),
            pl.BlockSpec((1, 1, 3 * D), lambda b, i: (b, 0, 0)),
            pl.BlockSpec((MIX_WIDTH, D), lambda b, i: (0, 0), pipeline_mode=pl.Buffered(1)),
        ],
        out_specs=pl.BlockSpec((1, tm, D), row),
        compiler_params=pltpu.CompilerParams(
            dimension_semantics=("arbitrary", "arbitrary"), vmem_limit_bytes=VMEM_LIMIT),
        name="out_proj",
    )(o_gla, o_diff, x, mod_l, w_out)


def _permute_w_in(w):
    idx = [0]
    for s in SPLIT_SIZES:
        idx.append(idx[-1] + s)
    gq, gk, gv, glr, gz, dq, dk, dv, dz = [w[:, idx[n]:idx[n + 1]] for n in range(9)]
    glr = jnp.pad(glr, ((0, 0), (0, GATE_PAD - GLA_GATE_RANK)))
    return jnp.concatenate([gq, gk, gv, gz, dq, dk, dv, dz, glr], axis=1).astype(BF16)


def kernel(x, c, w_ada, b_ada, norm_g, w_in, conv_w, w_gk, b_gk, gla_norm_g,
           qn_g, kn_g, lam_q1, lam_k1, lam_q2, lam_k2, diff_norm_g, w_out):
    B, S, D = x.shape
    mod = _modulation(c, w_ada, b_ada)

    r = jnp.arange(DIFF_QK) // DIFF_DH
    ones_bd = (r[:, None] == r[None, :]).astype(BF16)
    t = jnp.arange(GLA_T)
    tri = ((t[:, None] // CHUNK == t[None, :] // CHUNK)
           & (t[None, :] <= t[:, None])).astype(BF16)

    for l in range(DEPTH):
        mod_l = mod[l].reshape(B, 1, 3 * D)
        w_perm = _permute_w_in(w_in[l])
        qg = jnp.tile(qn_g[l], DIFF_QK // DIFF_DH).reshape(1, DIFF_QK)
        kg = jnp.tile(kn_g[l], DIFF_QK // DIFF_DH).reshape(1, DIFF_QK)
        gqkv, gz, dq, dk, dv, dz, glr = _in_projection(
            x, mod_l, norm_g[l].reshape(1, D), w_perm, ones_bd, qg, kg)

        conv_w8 = jnp.pad(conv_w[l], ((0, 8 - CONV_K), (0, 0)))
        wgk_pad = jnp.pad(w_gk[l], ((0, GATE_PAD - GLA_GATE_RANK), (0, 0))).astype(BF16)
        o_gla = _gla_branch(gqkv, glr, gz, conv_w8, wgk_pad, b_gk[l].reshape(1, GLA_QK),
                            tri, gla_norm_g[l].reshape(1, GLA_DV))

        lam_init = 0.8 - 0.6 * math.exp(-0.3 * l)
        o_diff = _diff_attention(
            dq, dk, dv, dz,
            lam_q1[l].reshape(1, DIFF_DH), lam_k1[l].reshape(1, DIFF_DH),
            lam_q2[l].reshape(1, DIFF_DH), lam_k2[l].reshape(1, DIFF_DH),
            jnp.full((1, LANES), lam_init, F32), diff_norm_g[l].reshape(1, DIFF_DV))

        x = _out_projection(o_gla, o_diff, x, mod_l, w_out[l].astype(BF16))
    return x
```

```python
import functools
import math

import jax
import jax.numpy as jnp
from jax import lax
from jax.experimental import pallas as pl
from jax.experimental.pallas import tpu as pltpu

D_MODEL = 1024
DEPTH = 4
CHUNK = 64
CONV_K = 4
EPS = 1e-6

GLA_HEADS = 4
GLA_DK = 64
GLA_DV = 128
GLA_QK = GLA_HEADS * GLA_DK
GLA_WIDTH = GLA_HEADS * GLA_DV
GLA_GATE_RANK = 16
GLA_GATE_TEMP = 16.0

DIFF_HEADS = 4
DIFF_DH = 64
DIFF_DV = 2 * DIFF_DH
DIFF_QK = DIFF_HEADS * 2 * DIFF_DH
DIFF_WIDTH = DIFF_HEADS * DIFF_DV

MIX_WIDTH = GLA_WIDTH + DIFF_WIDTH
GLA_CONV_WIDTH = 2 * GLA_QK + GLA_WIDTH
SPLIT_SIZES = (GLA_QK, GLA_QK, GLA_WIDTH, GLA_GATE_RANK, GLA_WIDTH,
               DIFF_QK, DIFF_QK, DIFF_WIDTH, DIFF_WIDTH)

LANES = 128
GATE_PAD = LANES
VMEM_LIMIT = 56 * 1024 * 1024

ATT_T = 512
PROJ_TM = ATT_T
OUT_TM = 512
GLA_T = 512
MOD_TN = 512

F32 = jnp.float32
BF16 = jnp.bfloat16


def _sigmoid(x):
    return 1.0 / (1.0 + jnp.exp(-x))


def _silu(x):
    return x * _sigmoid(x)


def _mod_kernel(ct_ref, w_ref, b_ref, o_ref):
    ct = ct_ref[...]
    ca = _silu(ct)
    w = w_ref[0]
    rows = []
    for b in range(ct.shape[1]):
        rows.append(jnp.sum(w * ca[:, b:b + 1], axis=0, keepdims=True))
    o_ref[0] = jnp.concatenate(rows, axis=0) + b_ref[0]


def _modulation(c, w_ada, b_ada):
    B, D = c.shape
    L, _, N = w_ada.shape
    return pl.pallas_call(
        _mod_kernel,
        out_shape=jax.ShapeDtypeStruct((L, B, N), F32),
        grid=(L, N // MOD_TN),
        in_specs=[
            pl.BlockSpec((D, B), lambda l, n: (0, 0)),
            pl.BlockSpec((1, D, MOD_TN), lambda l, n: (l, 0, n)),
            pl.BlockSpec((1, 1, MOD_TN), lambda l, n: (l, 0, n)),
        ],
        out_specs=pl.BlockSpec((1, B, MOD_TN), lambda l, n: (l, 0, n)),
        compiler_params=pltpu.CompilerParams(
            dimension_semantics=("arbitrary", "arbitrary"), vmem_limit_bytes=VMEM_LIMIT),
        name="adaln_mod",
    )(c.T, w_ada, b_ada.reshape(L, 1, N))


def _group_rms(t, ones_bd, gain):
    ss = jnp.dot((t * t).astype(BF16), ones_bd, preferred_element_type=F32)
    return t * lax.rsqrt(ss * (1.0 / DIFF_DH) + EPS) * gain


def _proj_kernel(x_ref, mod_ref, g_ref, w_ref, wt_ref, ones_ref, qg_ref, kg_ref,
                 gqkv_ref, gz_ref, dk_ref, dz_ref, glr_ref, qt_ref, vt_ref):
    D = D_MODEL
    x = x_ref[0]
    mod = mod_ref[0]
    shift = mod[:, 0:D]
    scale = mod[:, D:2 * D]
    ms = jnp.mean(x * x, axis=-1, keepdims=True)
    y = x * lax.rsqrt(ms + EPS) * g_ref[...]
    h = (y * (1.0 + scale) + shift).astype(BF16)

    def proj(a, n):
        return jnp.dot(h, w_ref[:, a:a + n], preferred_element_type=F32)

    def proj_t(a, n):
        return lax.dot_general(wt_ref[a:a + n, :], h, (((1,), (1,)), ((), ())),
                               preferred_element_type=F32)

    gqkv_ref[0, :, 0:512] = proj(0, 512).astype(BF16)
    gqkv_ref[0, :, 512:1024] = proj(512, 512).astype(BF16)
    gz_ref[0] = proj(1024, 512).astype(BF16)
    dk_ref[0] = _group_rms(proj(1536, 512), ones_ref[...], kg_ref[...]).astype(BF16)
    dz_ref[0] = proj(2048, 512).astype(BF16)
    glr_ref[0] = proj(2560, GATE_PAD).astype(BF16)

    qt = proj_t(0, DIFF_QK)
    for g in range(DIFF_QK // DIFF_DH):
        r0 = g * DIFF_DH
        t = qt[r0:r0 + DIFF_DH, :]
        ss = jnp.sum(t * t, axis=0, keepdims=True)
        t = t * lax.rsqrt(ss * (1.0 / DIFF_DH) + EPS) * qg_ref[r0:r0 + DIFF_DH, :]
        qt_ref[0, 0, r0:r0 + DIFF_DH, :] = (t * (DIFF_DH ** -0.5)).astype(BF16)
    vt_ref[0, 0] = proj_t(DIFF_QK, DIFF_WIDTH).astype(BF16)


def _in_projection(x, mod_l, norm_g, w_nat, w_t, ones_bd, qg_full, kg):
    B, S, D = x.shape
    tm = PROJ_TM
    NW = w_nat.shape[1]
    row = lambda b, i: (b, i, 0)
    tile = lambda b, i: (b, i, 0, 0)
    const = lambda b, i: (0, 0)
    widths = (1024, 512, 512, 512, GATE_PAD)
    once = pl.Buffered(1)
    return pl.pallas_call(
        _proj_kernel,
        out_shape=([jax.ShapeDtypeStruct((B, S, w), BF16) for w in widths]
                   + [jax.ShapeDtypeStruct((B, S // tm, DIFF_QK, tm), BF16),
                      jax.ShapeDtypeStruct((B, S // tm, DIFF_WIDTH, tm), BF16)]),
        grid=(B, S // tm),
        in_specs=[
            pl.BlockSpec((1, tm, D), row),
            pl.BlockSpec((1, 1, 3 * D), lambda b, i: (b, 0, 0)),
            pl.BlockSpec((1, D), const),
            pl.BlockSpec((D, NW), const, pipeline_mode=once),
            pl.BlockSpec((DIFF_QK + DIFF_WIDTH, D), const, pipeline_mode=once),
            pl.BlockSpec((512, 512), const, pipeline_mode=once),
            pl.BlockSpec((DIFF_QK, tm), const, pipeline_mode=once),
            pl.BlockSpec((1, 512), const),
        ],
        out_specs=([pl.BlockSpec((1, tm, w), row) for w in widths]
                   + [pl.BlockSpec((1, 1, DIFF_QK, tm), tile),
                      pl.BlockSpec((1, 1, DIFF_WIDTH, tm), tile)]),
        compiler_params=pltpu.CompilerParams(
            dimension_semantics=("arbitrary", "arbitrary"), vmem_limit_bytes=VMEM_LIMIT),
        name="in_proj",
    )(x, mod_l, norm_g, w_nat, w_t, ones_bd, qg_full, kg)


def _gla_kernel(u_ref, glr_ref, z_ref, cw_ref, wgk_ref, bgk_ref, tri_ref, g_ref,
                o_ref, xbuf, state):
    T = GLA_T
    t_idx = pl.program_id(1)

    @pl.when(t_idx == 0)
    def _():
        xbuf[0:8, :] = jnp.zeros((8, GLA_CONV_WIDTH), F32)
        state[...] = jnp.zeros(state.shape, F32)

    xbuf[8:8 + T, :] = u_ref[0].astype(F32)
    cw = cw_ref[...]
    conv = cw[0:1, :] * xbuf[pl.ds(8 - (CONV_K - 1), T), :]
    for j in range(1, CONV_K):
        conv = conv + cw[j:j + 1, :] * xbuf[pl.ds(8 - (CONV_K - 1) + j, T), :]
    xbuf[0:8, :] = xbuf[T:T + 8, :]
    act = _silu(conv)
    q = (act[:, 0:GLA_QK] * (GLA_DK ** -0.5)).astype(BF16)
    k = act[:, GLA_QK:2 * GLA_QK]
    v = act[:, 2 * GLA_QK:].astype(BF16)

    gate = jnp.dot(glr_ref[0], wgk_ref[...], preferred_element_type=F32) + bgk_ref[...]
    log_a = -(jnp.maximum(-gate, 0.0) + jnp.log1p(jnp.exp(-jnp.abs(gate))))
    log_a = log_a * (1.0 / GLA_GATE_TEMP)
    la_hi = log_a.astype(BF16)
    la_lo = (log_a - la_hi.astype(F32)).astype(BF16)
    tri = tri_ref[...]
    bcum = (jnp.dot(tri, la_hi, preferred_element_type=F32)
            + jnp.dot(tri, la_lo, preferred_element_type=F32))

    lane = lax.broadcasted_iota(jnp.int32, (CHUNK, LANES), 1)
    g_out = g_ref[...]
    for c in range(T // CHUNK):
        r0 = c * CHUNK
        b_c = bcum[r0:r0 + CHUNK, :]
        b_end = b_c[CHUNK - 1:CHUNK, :]
        kdec = k[r0:r0 + CHUNK, :] * jnp.exp(b_end - b_c)
        a_c = jnp.exp(b_end)
        for p in range(GLA_HEADS // 2):
            kd_pair = kdec[:, p * LANES:(p + 1) * LANES]
            q_pair = q[r0:r0 + CHUNK, p * LANES:(p + 1) * LANES]
            a_pair = a_c[:, p * LANES:(p + 1) * LANES]
            for hh in range(2):
                h = 2 * p + hh
                in_head = (lane >= hh * GLA_DK) & (lane < (hh + 1) * GLA_DK)
                kd_m = jnp.where(in_head, kd_pair, 0.0).astype(BF16)
                v_h = v[r0:r0 + CHUNK, h * GLA_DV:(h + 1) * GLA_DV]
                u_t = lax.dot_general(v_h, kd_m, (((0,), (0,)), ((), ())),
                                      preferred_element_type=F32)
                s_new = state[h] * a_pair + u_t
                state[h] = s_new
                o_h = lax.dot_general(q_pair, s_new.astype(BF16), (((1,), (1,)), ((), ())),
                                      preferred_element_type=F32)
                ms = jnp.mean(o_h * o_h, axis=-1, keepdims=True)
                zz = z_ref[0, r0:r0 + CHUNK, h * GLA_DV:(h + 1) * GLA_DV].astype(F32)
                res = o_h * lax.rsqrt(ms + EPS) * g_out * _silu(zz)
                o_ref[0, r0:r0 + CHUNK, h * GLA_DV:(h + 1) * GLA_DV] = res.astype(BF16)


def _gla_branch(gqkv, glr, gz, conv_w8, wgk_pad, bgk, tri, g_out):
    B, S, _ = gqkv.shape
    T = GLA_T
    row = lambda b, t: (b, t, 0)
    const = lambda b, t: (0, 0)
    return pl.pallas_call(
        _gla_kernel,
        out_shape=jax.ShapeDtypeStruct((B, S, GLA_WIDTH), BF16),
        grid=(B, S // T),
        in_specs=[
            pl.BlockSpec((1, T, GLA_CONV_WIDTH), row),
            pl.BlockSpec((1, T, GATE_PAD), row),
            pl.BlockSpec((1, T, GLA_WIDTH), row),
            pl.BlockSpec((8, GLA_CONV_WIDTH), const),
            pl.BlockSpec((GATE_PAD, GLA_QK), const),
            pl.BlockSpec((1, GLA_QK), const),
            pl.BlockSpec((T, T), const),
            pl.BlockSpec((1, GLA_DV), const),
        ],
        out_specs=pl.BlockSpec((1, T, GLA_WIDTH), row),
        scratch_shapes=[
            pltpu.VMEM((T + 8, GLA_CONV_WIDTH), F32),
            pltpu.VMEM((GLA_HEADS, GLA_DV, LANES), F32),
        ],
        compiler_params=pltpu.CompilerParams(
            dimension_semantics=("arbitrary", "arbitrary"), vmem_limit_bytes=VMEM_LIMIT),
        name="gla_branch",
    )(gqkv, glr, gz, conv_w8, wgk_pad, bgk, tri, g_out)


def _attn_kernel(qt_ref, k_ref, vt_ref, z_ref, lq1_ref, lk1_ref, lq2_ref, lk2_ref,
                 li_ref, g_ref, o_ref, s_a, s_b, m_s, l_s, acc_s):
    T = ATT_T
    i = pl.program_id(2)
    qt = qt_ref[0, 0]
    row = lax.broadcasted_iota(jnp.int32, (2 * DIFF_DH, T), 0)
    zero = jnp.zeros_like(qt)
    qt_maps = (jnp.where(row < DIFF_DH, qt, zero), jnp.where(row >= DIFF_DH, qt, zero))

    m_s[...] = jnp.full(m_s.shape, -jnp.inf, F32)
    l_s[...] = jnp.zeros(l_s.shape, F32)
    acc_s[...] = jnp.zeros(acc_s.shape, F32)

    def scores(j, s_ref):
        kj = k_ref[0, pl.ds(pl.multiple_of(j * T, T), T), :]
        for mp in range(2):
            s_ref[mp] = jnp.dot(kj, qt_maps[mp], preferred_element_type=F32)

    def update(j, s_ref, mask):
        vtj = vt_ref[0, j]
        for mp in range(2):
            s = s_ref[mp]
            if mask is not None:
                s = jnp.where(mask, s, -jnp.inf)
            m_old = m_s[mp]
            m_new = jnp.maximum(m_old, jnp.max(s, axis=0, keepdims=True))
            alpha = jnp.exp(m_old - m_new)
            p = jnp.exp(s - m_new)
            l_s[mp] = alpha * l_s[mp] + jnp.sum(p, axis=0, keepdims=True)
            acc_s[mp] = alpha * acc_s[mp] + jnp.dot(vtj, p.astype(BF16),
                                                    preferred_element_type=F32)
            m_s[mp] = m_new

    scores(0, s_a)

    def pair(t, carry):
        j = 2 * t
        update(j, s_a, None)
        scores(j + 1, s_b)
        update(j + 1, s_b, None)
        scores(j + 2, s_a)
        return carry

    lax.fori_loop(0, i // 2, pair, 0)

    kc = lax.broadcasted_iota(jnp.int32, (T, T), 0) // CHUNK
    qc = lax.broadcasted_iota(jnp.int32, (T, T), 1) // CHUNK
    mask = kc <= qc

    @pl.when(i % 2 == 0)
    def _():
        update(i, s_a, mask)

    @pl.when(i % 2 == 1)
    def _():
        scores(i, s_b)
        update(i - 1, s_a, None)
        update(i, s_b, mask)

    lam_init = li_ref[:, 0:1]
    lam = (jnp.exp(jnp.sum(lq1_ref[...] * lk1_ref[...], axis=-1, keepdims=True))
           - jnp.exp(jnp.sum(lq2_ref[...] * lk2_ref[...], axis=-1, keepdims=True))
           + lam_init)
    out = acc_s[0] / l_s[0] - lam * (acc_s[1] / l_s[1])
    ms = jnp.mean(out * out, axis=0, keepdims=True)
    out = out * lax.rsqrt(ms + EPS) * g_ref[...] * (1.0 - lam_init)
    o_ref[0] = (out.T * _silu(z_ref[0].astype(F32))).astype(BF16)


def _diff_attention(qt, dk, vt, dz, lq1, lk1, lq2, lk2, lam_init_row, g_full):
    B, S, _ = dk.shape
    T = ATT_T
    H = DIFF_HEADS
    nt = S // T
    qmap = lambda b, h, i: (b, i, h)
    const = lambda b, h, i: (0, 0)
    return pl.pallas_call(
        _attn_kernel,
        out_shape=jax.ShapeDtypeStruct((B, S, DIFF_WIDTH), BF16),
        grid=(B, H, nt),
        in_specs=[
            pl.BlockSpec((1, 1, 2 * DIFF_DH, T), lambda b, h, i: (b, i, h, 0)),
            pl.BlockSpec((1, S, LANES), lambda b, h, i: (b, 0, h)),
            pl.BlockSpec((1, nt, DIFF_DV, T), lambda b, h, i: (b, 0, h, 0)),
            pl.BlockSpec((1, T, LANES), qmap),
            pl.BlockSpec((1, DIFF_DH), const),
            pl.BlockSpec((1, DIFF_DH), const),
            pl.BlockSpec((1, DIFF_DH), const),
            pl.BlockSpec((1, DIFF_DH), const),
            pl.BlockSpec((1, LANES), const),
            pl.BlockSpec((DIFF_DV, T), const),
        ],
        out_specs=pl.BlockSpec((1, T, LANES), qmap),
        scratch_shapes=[
            pltpu.VMEM((2, T, T), F32),
            pltpu.VMEM((2, T, T), F32),
            pltpu.VMEM((2, 1, T), F32),
            pltpu.VMEM((2, 1, T), F32),
            pltpu.VMEM((2, DIFF_DV, T), F32),
        ],
        compiler_params=pltpu.CompilerParams(
            dimension_semantics=("arbitrary", "arbitrary", "arbitrary"),
            vmem_limit_bytes=VMEM_LIMIT),
        name="diff_attn",
    )(qt, dk, vt, dz, lq1, lk1, lq2, lk2, lam_init_row, g_full)


def _out_kernel(og_ref, od_ref, x_ref, mod_ref, w_ref, o_ref):
    D = D_MODEL
    gate = mod_ref[0][:, 2 * D:3 * D]
    y = (jnp.dot(og_ref[0], w_ref[0:GLA_WIDTH, :], preferred_element_type=F32)
         + jnp.dot(od_ref[0], w_ref[GLA_WIDTH:MIX_WIDTH, :], preferred_element_type=F32))
    o_ref[0] = x_ref[0] + gate * y


def _out_projection(o_gla, o_diff, x, mod_l, w_out):
    B, S, D = x.shape
    tm = OUT_TM
    row = lambda b, i: (b, i, 0)
    return pl.pallas_call(
        _out_kernel,
        out_shape=jax.ShapeDtypeStruct((B, S, D), F32),
        grid=(B, S // tm),
        in_specs=[
            pl.BlockSpec((1, tm, GLA_WIDTH), row),
            pl.BlockSpec((1, tm, DIFF_WIDTH), row),
            pl.BlockSpec((1, tm, D), row),
            pl.BlockSpec((1, 1, 3 * D), lambda b, i: (b, 0, 0)),
            pl.BlockSpec((MIX_WIDTH, D), lambda b, i: (0, 0), pipeline_mode=pl.Buffered(1)),
        ],
        out_specs=pl.BlockSpec((1, tm, D), row),
        compiler_params=pltpu.CompilerParams(
            dimension_semantics=("arbitrary", "arbitrary"), vmem_limit_bytes=VMEM_LIMIT),
        name="out_proj",
    )(o_gla, o_diff, x, mod_l, w_out)


def _split_w_in(w):
    idx = [0]
    for s in SPLIT_SIZES:
        idx.append(idx[-1] + s)
    gq, gk, gv, glr, gz, dq, dk, dv, dz = [w[:, idx[n]:idx[n + 1]] for n in range(9)]
    glr = jnp.pad(glr, ((0, 0), (0, GATE_PAD - GLA_GATE_RANK)))
    w_nat = jnp.concatenate([gq, gk, gv, gz, dk, dz, glr], axis=1).astype(BF16)
    w_t = jnp.concatenate([dq, dv], axis=1).T.astype(BF16)
    return w_nat, w_t


def kernel(x, c, w_ada, b_ada, norm_g, w_in, conv_w, w_gk, b_gk, gla_norm_g,
           qn_g, kn_g, lam_q1, lam_k1, lam_q2, lam_k2, diff_norm_g, w_out):
    B, S, D = x.shape
    mod = _modulation(c, w_ada, b_ada)

    r = jnp.arange(DIFF_QK) // DIFF_DH
    ones_bd = (r[:, None] == r[None, :]).astype(BF16)
    t = jnp.arange(GLA_T)
    tri = ((t[:, None] // CHUNK == t[None, :] // CHUNK)
           & (t[None, :] <= t[:, None])).astype(BF16)

    for l in range(DEPTH):
        mod_l = mod[l].reshape(B, 1, 3 * D)
        w_nat, w_t = _split_w_in(w_in[l])
        qg_full = jnp.broadcast_to(
            jnp.tile(qn_g[l], DIFF_QK // DIFF_DH)[:, None], (DIFF_QK, PROJ_TM))
        kg = jnp.tile(kn_g[l], DIFF_QK // DIFF_DH).reshape(1, DIFF_QK)
        gqkv, gz, dk, dz, glr, qt, vt = _in_projection(
            x, mod_l, norm_g[l].reshape(1, D), w_nat, w_t, ones_bd, qg_full, kg)

        conv_w8 = jnp.pad(conv_w[l], ((0, 8 - CONV_K), (0, 0)))
        wgk_pad = jnp.pad(w_gk[l], ((0, GATE_PAD - GLA_GATE_RANK), (0, 0))).astype(BF16)
        o_gla = _gla_branch(gqkv, glr, gz, conv_w8, wgk_pad, b_gk[l].reshape(1, GLA_QK),
                            tri, gla_norm_g[l].reshape(1, GLA_DV))

        lam_init = 0.8 - 0.6 * math.exp(-0.3 * l)
        o_diff = _diff_attention(
            qt, dk, vt, dz,
            lam_q1[l].reshape(1, DIFF_DH), lam_k1[l].reshape(1, DIFF_DH),
            lam_q2[l].reshape(1, DIFF_DH), lam_k2[l].reshape(1, DIFF_DH),
            jnp.full((1, LANES), lam_init, F32),
            jnp.broadcast_to(diff_norm_g[l][:, None], (DIFF_DV, ATT_T)))

        x = _out_projection(o_gla, o_diff, x, mod_l, w_out[l].astype(BF16))
    return x
```

```python
import functools
import math

import jax
import jax.numpy as jnp
from jax import lax
from jax.experimental import pallas as pl
from jax.experimental.pallas import tpu as pltpu

D_MODEL = 1024
DEPTH = 4
CHUNK = 64
CONV_K = 4
EPS = 1e-6

GLA_HEADS = 4
GLA_DK = 64
GLA_DV = 128
GLA_QK = GLA_HEADS * GLA_DK
GLA_WIDTH = GLA_HEADS * GLA_DV
GLA_GATE_RANK = 16
GLA_GATE_TEMP = 16.0

DIFF_HEADS = 4
DIFF_DH = 64
DIFF_DV = 2 * DIFF_DH
DIFF_QK = DIFF_HEADS * 2 * DIFF_DH
DIFF_WIDTH = DIFF_HEADS * DIFF_DV

MIX_WIDTH = GLA_WIDTH + DIFF_WIDTH
GLA_CONV_WIDTH = 2 * GLA_QK + GLA_WIDTH
SPLIT_SIZES = (GLA_QK, GLA_QK, GLA_WIDTH, GLA_GATE_RANK, GLA_WIDTH,
               DIFF_QK, DIFF_QK, DIFF_WIDTH, DIFF_WIDTH)

LANES = 128
GATE_PAD = LANES
VMEM_LIMIT = 56 * 1024 * 1024

ATT_T = 512
ATT_LROWS = 16
PROJ_TM = ATT_T
OUT_TM = 512
GLA_T = 512
MOD_TN = 512

F32 = jnp.float32
BF16 = jnp.bfloat16
LOG2E = math.log2(math.e)


def _sigmoid(x):
    return 1.0 / (1.0 + jnp.exp(-x))


def _silu(x):
    return x * _sigmoid(x)


def _mod_kernel(ct_ref, w_ref, b_ref, o_ref):
    ct = ct_ref[...]
    ca = _silu(ct)
    w = w_ref[0]
    rows = []
    for b in range(ct.shape[1]):
        rows.append(jnp.sum(w * ca[:, b:b + 1], axis=0, keepdims=True))
    o_ref[0] = jnp.concatenate(rows, axis=0) + b_ref[0]


def _modulation(c, w_ada, b_ada):
    B, D = c.shape
    L, _, N = w_ada.shape
    return pl.pallas_call(
        _mod_kernel,
        out_shape=jax.ShapeDtypeStruct((L, B, N), F32),
        grid=(L, N // MOD_TN),
        in_specs=[
            pl.BlockSpec((D, B), lambda l, n: (0, 0)),
            pl.BlockSpec((1, D, MOD_TN), lambda l, n: (l, 0, n)),
            pl.BlockSpec((1, 1, MOD_TN), lambda l, n: (l, 0, n)),
        ],
        out_specs=pl.BlockSpec((1, B, MOD_TN), lambda l, n: (l, 0, n)),
        compiler_params=pltpu.CompilerParams(
            dimension_semantics=("arbitrary", "arbitrary"), vmem_limit_bytes=VMEM_LIMIT),
        name="adaln_mod",
    )(c.T, w_ada, b_ada.reshape(L, 1, N))


def _group_rms(t, ones_bd, gain):
    ss = jnp.dot((t * t).astype(BF16), ones_bd, preferred_element_type=F32)
    return t * lax.rsqrt(ss * (1.0 / DIFF_DH) + EPS) * gain


def _proj_kernel(x_ref, mod_ref, g_ref, w_ref, wt_ref, ones_ref, qg_ref, kg_ref,
                 gqkv_ref, gz_ref, dk_ref, dz_ref, glr_ref, qt_ref, vt_ref):
    D = D_MODEL
    x = x_ref[0]
    mod = mod_ref[0]
    shift = mod[:, 0:D]
    scale = mod[:, D:2 * D]
    ms = jnp.mean(x * x, axis=-1, keepdims=True)
    y = x * lax.rsqrt(ms + EPS) * g_ref[...]
    h = (y * (1.0 + scale) + shift).astype(BF16)

    def proj(a, n):
        return jnp.dot(h, w_ref[:, a:a + n], preferred_element_type=F32)

    def proj_t(a, n):
        return lax.dot_general(wt_ref[a:a + n, :], h, (((1,), (1,)), ((), ())),
                               preferred_element_type=F32)

    gqkv_ref[0, :, 0:512] = proj(0, 512).astype(BF16)
    gqkv_ref[0, :, 512:1024] = proj(512, 512).astype(BF16)
    gz_ref[0] = proj(1024, 512).astype(BF16)
    dk_ref[0] = _group_rms(proj(1536, 512), ones_ref[...], kg_ref[...]).astype(BF16)
    dz_ref[0] = proj(2048, 512).astype(BF16)
    glr_ref[0] = proj(2560, GATE_PAD).astype(BF16)

    qt = proj_t(0, DIFF_QK)
    for g in range(DIFF_QK // DIFF_DH):
        r0 = g * DIFF_DH
        t = qt[r0:r0 + DIFF_DH, :]
        ss = jnp.sum(t * t, axis=0, keepdims=True)
        t = t * lax.rsqrt(ss * (1.0 / DIFF_DH) + EPS) * qg_ref[r0:r0 + DIFF_DH, :]
        qt_ref[0, 0, r0:r0 + DIFF_DH, :] = (t * (DIFF_DH ** -0.5 * LOG2E)).astype(BF16)
    vt_ref[0, 0] = proj_t(DIFF_QK, DIFF_WIDTH).astype(BF16)


def _in_projection(x, mod_l, norm_g, w_nat, w_t, ones_bd, qg_full, kg):
    B, S, D = x.shape
    tm = PROJ_TM
    NW = w_nat.shape[1]
    row = lambda b, i: (b, i, 0)
    tile = lambda b, i: (b, i, 0, 0)
    const = lambda b, i: (0, 0)
    widths = (1024, 512, 512, 512, GATE_PAD)
    once = pl.Buffered(1)
    return pl.pallas_call(
        _proj_kernel,
        out_shape=([jax.ShapeDtypeStruct((B, S, w), BF16) for w in widths]
                   + [jax.ShapeDtypeStruct((B, S // tm, DIFF_QK, tm), BF16),
                      jax.ShapeDtypeStruct((B, S // tm, DIFF_WIDTH, tm), BF16)]),
        grid=(B, S // tm),
        in_specs=[
            pl.BlockSpec((1, tm, D), row),
            pl.BlockSpec((1, 1, 3 * D), lambda b, i: (b, 0, 0)),
            pl.BlockSpec((1, D), const),
            pl.BlockSpec((D, NW), const, pipeline_mode=once),
            pl.BlockSpec((DIFF_QK + DIFF_WIDTH, D), const, pipeline_mode=once),
            pl.BlockSpec((512, 512), const, pipeline_mode=once),
            pl.BlockSpec((DIFF_QK, tm), const, pipeline_mode=once),
            pl.BlockSpec((1, 512), const),
        ],
        out_specs=([pl.BlockSpec((1, tm, w), row) for w in widths]
                   + [pl.BlockSpec((1, 1, DIFF_QK, tm), tile),
                      pl.BlockSpec((1, 1, DIFF_WIDTH, tm), tile)]),
        compiler_params=pltpu.CompilerParams(
            dimension_semantics=("arbitrary", "arbitrary"), vmem_limit_bytes=VMEM_LIMIT),
        name="in_proj",
    )(x, mod_l, norm_g, w_nat, w_t, ones_bd, qg_full, kg)


def _gla_kernel(u_ref, glr_ref, z_ref, cw_ref, wgk_ref, bgk_ref, tri_ref, g_ref,
                o_ref, xbuf, state):
    T = GLA_T
    t_idx = pl.program_id(1)

    @pl.when(t_idx == 0)
    def _():
        xbuf[0:8, :] = jnp.zeros((8, GLA_CONV_WIDTH), F32)
        state[...] = jnp.zeros(state.shape, F32)

    xbuf[8:8 + T, :] = u_ref[0].astype(F32)
    cw = cw_ref[...]
    conv = cw[0:1, :] * xbuf[pl.ds(8 - (CONV_K - 1), T), :]
    for j in range(1, CONV_K):
        conv = conv + cw[j:j + 1, :] * xbuf[pl.ds(8 - (CONV_K - 1) + j, T), :]
    xbuf[0:8, :] = xbuf[T:T + 8, :]
    act = _silu(conv)
    q = (act[:, 0:GLA_QK] * (GLA_DK ** -0.5)).astype(BF16)
    k = act[:, GLA_QK:2 * GLA_QK]
    v = act[:, 2 * GLA_QK:].astype(BF16)

    gate = jnp.dot(glr_ref[0], wgk_ref[...], preferred_element_type=F32) + bgk_ref[...]
    log_a = -(jnp.maximum(-gate, 0.0) + jnp.log1p(jnp.exp(-jnp.abs(gate))))
    log_a = log_a * (1.0 / GLA_GATE_TEMP)
    la_hi = log_a.astype(BF16)
    la_lo = (log_a - la_hi.astype(F32)).astype(BF16)
    tri = tri_ref[...]
    bcum = (jnp.dot(tri, la_hi, preferred_element_type=F32)
            + jnp.dot(tri, la_lo, preferred_element_type=F32))

    lane = lax.broadcasted_iota(jnp.int32, (CHUNK, LANES), 1)
    g_out = g_ref[...]
    for c in range(T // CHUNK):
        r0 = c * CHUNK
        b_c = bcum[r0:r0 + CHUNK, :]
        b_end = b_c[CHUNK - 1:CHUNK, :]
        kdec = k[r0:r0 + CHUNK, :] * jnp.exp(b_end - b_c)
        a_c = jnp.exp(b_end)
        for p in range(GLA_HEADS // 2):
            kd_pair = kdec[:, p * LANES:(p + 1) * LANES]
            q_pair = q[r0:r0 + CHUNK, p * LANES:(p + 1) * LANES]
            a_pair = a_c[:, p * LANES:(p + 1) * LANES]
            for hh in range(2):
                h = 2 * p + hh
                in_head = (lane >= hh * GLA_DK) & (lane < (hh + 1) * GLA_DK)
                kd_m = jnp.where(in_head, kd_pair, 0.0).astype(BF16)
                v_h = v[r0:r0 + CHUNK, h * GLA_DV:(h + 1) * GLA_DV]
                u_t = lax.dot_general(v_h, kd_m, (((0,), (0,)), ((), ())),
                                      preferred_element_type=F32)
                s_new = state[h] * a_pair + u_t
                state[h] = s_new
                o_h = lax.dot_general(q_pair, s_new.astype(BF16), (((1,), (1,)), ((), ())),
                                      preferred_element_type=F32)
                ms = jnp.mean(o_h * o_h, axis=-1, keepdims=True)
                zz = z_ref[0, r0:r0 + CHUNK, h * GLA_DV:(h + 1) * GLA_DV].astype(F32)
                res = o_h * lax.rsqrt(ms + EPS) * g_out * _silu(zz)
                o_ref[0, r0:r0 + CHUNK, h * GLA_DV:(h + 1) * GLA_DV] = res.astype(BF16)


def _gla_branch(gqkv, glr, gz, conv_w8, wgk_pad, bgk, tri, g_out):
    B, S, _ = gqkv.shape
    T = GLA_T
    row = lambda b, t: (b, t, 0)
    const = lambda b, t: (0, 0)
    return pl.pallas_call(
        _gla_kernel,
        out_shape=jax.ShapeDtypeStruct((B, S, GLA_WIDTH), BF16),
        grid=(B, S // T),
        in_specs=[
            pl.BlockSpec((1, T, GLA_CONV_WIDTH), row),
            pl.BlockSpec((1, T, GATE_PAD), row),
            pl.BlockSpec((1, T, GLA_WIDTH), row),
            pl.BlockSpec((8, GLA_CONV_WIDTH), const),
            pl.BlockSpec((GATE_PAD, GLA_QK), const),
            pl.BlockSpec((1, GLA_QK), const),
            pl.BlockSpec((T, T), const),
            pl.BlockSpec((1, GLA_DV), const),
        ],
        out_specs=pl.BlockSpec((1, T, GLA_WIDTH), row),
        scratch_shapes=[
            pltpu.VMEM((T + 8, GLA_CONV_WIDTH), F32),
            pltpu.VMEM((GLA_HEADS, GLA_DV, LANES), F32),
        ],
        compiler_params=pltpu.CompilerParams(
            dimension_semantics=("arbitrary", "arbitrary"), vmem_limit_bytes=VMEM_LIMIT),
        name="gla_branch",
    )(gqkv, glr, gz, conv_w8, wgk_pad, bgk, tri, g_out)


def _attn_kernel(qt_ref, k_ref, vt_ref, z_ref, lq1_ref, lk1_ref, lq2_ref, lk2_ref,
                 li_ref, g_ref, o_ref, s_a, s_b, m_s, acc_s):
    T = ATT_T
    i = pl.program_id(2)
    qt = qt_ref[0, 0]
    row = lax.broadcasted_iota(jnp.int32, (2 * DIFF_DH, T), 0)
    zero = jnp.zeros_like(qt)
    qt_maps = (jnp.where(row < DIFF_DH, qt, zero), jnp.where(row >= DIFF_DH, qt, zero))

    m_s[...] = jnp.full(m_s.shape, -jnp.inf, F32)
    acc_s[...] = jnp.zeros(acc_s.shape, F32)

    def scores(j, s_ref):
        kj = k_ref[0, pl.ds(pl.multiple_of(j * T, T), T), :]
        for mp in range(2):
            s_ref[mp] = jnp.dot(kj, qt_maps[mp], preferred_element_type=F32)

    def update(j, s_ref, mask):
        vtj = jnp.concatenate([vt_ref[0, j], jnp.ones((ATT_LROWS, T), BF16)], axis=0)
        for mp in range(2):
            s = s_ref[mp]
            if mask is not None:
                s = jnp.where(mask, s, -jnp.inf)
            m_old = m_s[mp]
            m_new = jnp.maximum(m_old, jnp.max(s, axis=0, keepdims=True))
            alpha = jnp.exp2(m_old - m_new)
            p = jnp.exp2(s - m_new)
            acc_s[mp] = alpha * acc_s[mp] + jnp.dot(vtj, p.astype(BF16),
                                                    preferred_element_type=F32)
            m_s[mp] = m_new

    scores(0, s_a)

    def pair(t, carry):
        j = 2 * t
        scores(j + 1, s_b)
        update(j, s_a, None)
        scores(j + 2, s_a)
        update(j + 1, s_b, None)
        return carry

    lax.fori_loop(0, i // 2, pair, 0)

    kc = lax.broadcasted_iota(jnp.int32, (T, T), 0) // CHUNK
    qc = lax.broadcasted_iota(jnp.int32, (T, T), 1) // CHUNK
    mask = kc <= qc

    @pl.when(i % 2 == 0)
    def _():
        update(i, s_a, mask)

    @pl.when(i % 2 == 1)
    def _():
        scores(i, s_b)
        update(i - 1, s_a, None)
        update(i, s_b, mask)

    lam_init = li_ref[:, 0:1]
    lam = (jnp.exp(jnp.sum(lq1_ref[...] * lk1_ref[...], axis=-1, keepdims=True))
           - jnp.exp(jnp.sum(lq2_ref[...] * lk2_ref[...], axis=-1, keepdims=True))
           + lam_init)
    acc0, acc1 = acc_s[0], acc_s[1]
    out = (acc0[0:DIFF_DV] / acc0[DIFF_DV:DIFF_DV + 1]
           - lam * (acc1[0:DIFF_DV] / acc1[DIFF_DV:DIFF_DV + 1]))
    ms = jnp.mean(out * out, axis=0, keepdims=True)
    out = out * lax.rsqrt(ms + EPS) * g_ref[...] * (1.0 - lam_init)
    o_ref[0] = (out.T * _silu(z_ref[0].astype(F32))).astype(BF16)


def _diff_attention(qt, dk, vt, dz, lq1, lk1, lq2, lk2, lam_init_row, g_full):
    B, S, _ = dk.shape
    T = ATT_T
    H = DIFF_HEADS
    nt = S // T
    qmap = lambda b, h, i: (b, i, h)
    const = lambda b, h, i: (0, 0)
    return pl.pallas_call(
        _attn_kernel,
        out_shape=jax.ShapeDtypeStruct((B, S, DIFF_WIDTH), BF16),
        grid=(B, H, nt),
        in_specs=[
            pl.BlockSpec((1, 1, 2 * DIFF_DH, T), lambda b, h, i: (b, i, h, 0)),
            pl.BlockSpec((1, S, LANES), lambda b, h, i: (b, 0, h)),
            pl.BlockSpec((1, nt, DIFF_DV, T), lambda b, h, i: (b, 0, h, 0)),
            pl.BlockSpec((1, T, LANES), qmap),
            pl.BlockSpec((1, DIFF_DH), const),
            pl.BlockSpec((1, DIFF_DH), const),
            pl.BlockSpec((1, DIFF_DH), const),
            pl.BlockSpec((1, DIFF_DH), const),
            pl.BlockSpec((1, LANES), const),
            pl.BlockSpec((DIFF_DV, T), const),
        ],
        out_specs=pl.BlockSpec((1, T, LANES), qmap),
        scratch_shapes=[
            pltpu.VMEM((2, T, T), F32),
            pltpu.VMEM((2, T, T), F32),
            pltpu.VMEM((2, 1, T), F32),
            pltpu.VMEM((2, DIFF_DV + ATT_LROWS, T), F32),
        ],
        compiler_params=pltpu.CompilerParams(
            dimension_semantics=("arbitrary", "arbitrary", "arbitrary"),
            vmem_limit_bytes=VMEM_LIMIT),
        name="diff_attn",
    )(qt, dk, vt, dz, lq1, lk1, lq2, lk2, lam_init_row, g_full)


def _out_kernel(og_ref, od_ref, x_ref, mod_ref, w_ref, o_ref):
    D = D_MODEL
    gate = mod_ref[0][:, 2 * D:3 * D]
    y = (jnp.dot(og_ref[0], w_ref[0:GLA_WIDTH, :], preferred_element_type=F32)
         + jnp.dot(od_ref[0], w_ref[GLA_WIDTH:MIX_WIDTH, :], preferred_element_type=F32))
    o_ref[0] = x_ref[0] + gate * y


def _out_projection(o_gla, o_diff, x, mod_l, w_out):
    B, S, D = x.shape
    tm = OUT_TM
    row = lambda b, i: (b, i, 0)
    return pl.pallas_call(
        _out_kernel,
        out_shape=jax.ShapeDtypeStruct((B, S, D), F32),
        grid=(B, S // tm),
        in_specs=[
            pl.BlockSpec((1, tm, GLA_WIDTH), row),
            pl.BlockSpec((1, tm, DIFF_WIDTH), row),
            pl.BlockSpec((1, tm, D), row),
            pl.BlockSpec((1, 1, 3 * D), lambda b, i: (b, 0, 0)),
            pl.BlockSpec((MIX_WIDTH, D), lambda b, i: (0, 0), pipeline_mode=pl.Buffered(1)),
        ],
        out_specs=pl.BlockSpec((1, tm, D), row),
        compiler_params=pltpu.CompilerParams(
            dimension_semantics=("arbitrary", "arbitrary"), vmem_limit_bytes=VMEM_LIMIT),
        name="out_proj",
    )(o_gla, o_diff, x, mod_l, w_out)


def _split_w_in(w):
    idx = [0]
    for s in SPLIT_SIZES:
        idx.append(idx[-1] + s)
    gq, gk, gv, glr, gz, dq, dk, dv, dz = [w[:, idx[n]:idx[n + 1]] for n in range(9)]
    glr = jnp.pad(glr, ((0, 0), (0, GATE_PAD - GLA_GATE_RANK)))
    w_nat = jnp.concatenate([gq, gk, gv, gz, dk, dz, glr], axis=1).astype(BF16)
    w_t = jnp.concatenate([dq, dv], axis=1).T.astype(BF16)
    return w_nat, w_t


def kernel(x, c, w_ada, b_ada, norm_g, w_in, conv_w, w_gk, b_gk, gla_norm_g,
           qn_g, kn_g, lam_q1, lam_k1, lam_q2, lam_k2, diff_norm_g, w_out):
    B, S, D = x.shape
    mod = _modulation(c, w_ada, b_ada)

    r = jnp.arange(DIFF_QK) // DIFF_DH
    ones_bd = (r[:, None] == r[None, :]).astype(BF16)
    t = jnp.arange(GLA_T)
    tri = ((t[:, None] // CHUNK == t[None, :] // CHUNK)
           & (t[None, :] <= t[:, None])).astype(BF16)

    for l in range(DEPTH):
        mod_l = mod[l].reshape(B, 1, 3 * D)
        w_nat, w_t = _split_w_in(w_in[l])
        qg_full = jnp.broadcast_to(
            jnp.tile(qn_g[l], DIFF_QK // DIFF_DH)[:, None], (DIFF_QK, PROJ_TM))
        kg = jnp.tile(kn_g[l], DIFF_QK // DIFF_DH).reshape(1, DIFF_QK)
        gqkv, gz, dk, dz, glr, qt, vt = _in_projection(
            x, mod_l, norm_g[l].reshape(1, D), w_nat, w_t, ones_bd, qg_full, kg)

        conv_w8 = jnp.pad(conv_w[l], ((0, 8 - CONV_K), (0, 0)))
        wgk_pad = jnp.pad(w_gk[l], ((0, GATE_PAD - GLA_GATE_RANK), (0, 0))).astype(BF16)
        o_gla = _gla_branch(gqkv, glr, gz, conv_w8, wgk_pad, b_gk[l].reshape(1, GLA_QK),
                            tri, gla_norm_g[l].reshape(1, GLA_DV))

        lam_init = 0.8 - 0.6 * math.exp(-0.3 * l)
        o_diff = _diff_attention(
            qt, dk, vt, dz,
            lam_q1[l].reshape(1, DIFF_DH), lam_k1[l].reshape(1, DIFF_DH),
            lam_q2[l].reshape(1, DIFF_DH), lam_k2[l].reshape(1, DIFF_DH),
            jnp.full((1, LANES), lam_init, F32),
            jnp.broadcast_to(diff_norm_g[l][:, None], (DIFF_DV, ATT_T)))

        x = _out_projection(o_gla, o_diff, x, mod_l, w_out[l].astype(BF16))
    return x
```

```python
import functools
import math

import jax
import jax.numpy as jnp
from jax import lax
from jax.experimental import pallas as pl
from jax.experimental.pallas import tpu as pltpu

D_MODEL = 1024
DEPTH = 4
CHUNK = 64
CONV_K = 4
EPS = 1e-6

GLA_HEADS = 4
GLA_DK = 64
GLA_DV = 128
GLA_QK = GLA_HEADS * GLA_DK
GLA_WIDTH = GLA_HEADS * GLA_DV
GLA_GATE_RANK = 16
GLA_GATE_TEMP = 16.0

DIFF_HEADS = 4
DIFF_DH = 64
DIFF_DV = 2 * DIFF_DH
DIFF_QK = DIFF_HEADS * 2 * DIFF_DH
DIFF_WIDTH = DIFF_HEADS * DIFF_DV

MIX_WIDTH = GLA_WIDTH + DIFF_WIDTH
GLA_CONV_WIDTH = 2 * GLA_QK + GLA_WIDTH
SPLIT_SIZES = (GLA_QK, GLA_QK, GLA_WIDTH, GLA_GATE_RANK, GLA_WIDTH,
               DIFF_QK, DIFF_QK, DIFF_WIDTH, DIFF_WIDTH)

LANES = 128
GATE_PAD = LANES
VMEM_LIMIT = 56 * 1024 * 1024

ATT_T = 512
ATT_SPAD = 128
ATT_LROWS = 16
PROJ_TM = ATT_T
OUT_TM = 512
GLA_T = 512
MOD_TN = 512

F32 = jnp.float32
BF16 = jnp.bfloat16
LOG2E = math.log2(math.e)


def _sigmoid(x):
    return 1.0 / (1.0 + jnp.exp(-x))


def _silu(x):
    return x * _sigmoid(x)


def _mod_kernel(ct_ref, w_ref, b_ref, o_ref):
    ct = ct_ref[...]
    ca = _silu(ct)
    w = w_ref[0]
    rows = []
    for b in range(ct.shape[1]):
        rows.append(jnp.sum(w * ca[:, b:b + 1], axis=0, keepdims=True))
    o_ref[0] = jnp.concatenate(rows, axis=0) + b_ref[0]


def _modulation(c, w_ada, b_ada):
    B, D = c.shape
    L, _, N = w_ada.shape
    return pl.pallas_call(
        _mod_kernel,
        out_shape=jax.ShapeDtypeStruct((L, B, N), F32),
        grid=(L, N // MOD_TN),
        in_specs=[
            pl.BlockSpec((D, B), lambda l, n: (0, 0)),
            pl.BlockSpec((1, D, MOD_TN), lambda l, n: (l, 0, n)),
            pl.BlockSpec((1, 1, MOD_TN), lambda l, n: (l, 0, n)),
        ],
        out_specs=pl.BlockSpec((1, B, MOD_TN), lambda l, n: (l, 0, n)),
        compiler_params=pltpu.CompilerParams(
            dimension_semantics=("arbitrary", "arbitrary"), vmem_limit_bytes=VMEM_LIMIT),
        name="adaln_mod",
    )(c.T, w_ada, b_ada.reshape(L, 1, N))


def _group_rms(t, ones_bd, gain):
    ss = jnp.dot((t * t).astype(BF16), ones_bd, preferred_element_type=F32)
    return t * lax.rsqrt(ss * (1.0 / DIFF_DH) + EPS) * gain


def _proj_kernel(x_ref, mod_ref, g_ref, w_ref, wt_ref, ones_ref, qg_ref, kg_ref,
                 gqkv_ref, gz_ref, dk_ref, dz_ref, glr_ref, qt_ref, vt_ref):
    D = D_MODEL
    x = x_ref[0]
    mod = mod_ref[0]
    shift = mod[:, 0:D]
    scale = mod[:, D:2 * D]
    ms = jnp.mean(x * x, axis=-1, keepdims=True)
    y = x * lax.rsqrt(ms + EPS) * g_ref[...]
    h = (y * (1.0 + scale) + shift).astype(BF16)

    def proj(a, n):
        return jnp.dot(h, w_ref[:, a:a + n], preferred_element_type=F32)

    def proj_t(a, n):
        return lax.dot_general(wt_ref[a:a + n, :], h, (((1,), (1,)), ((), ())),
                               preferred_element_type=F32)

    gqkv_ref[0, :, 0:512] = proj(0, 512).astype(BF16)
    gqkv_ref[0, :, 512:1024] = proj(512, 512).astype(BF16)
    gz_ref[0] = proj(1024, 512).astype(BF16)
    dk_ref[0] = _group_rms(proj(1536, 512), ones_ref[...], kg_ref[...]).astype(BF16)
    dz_ref[0] = proj(2048, 512).astype(BF16)
    glr_ref[0] = proj(2560, GATE_PAD).astype(BF16)

    qt = proj_t(0, DIFF_QK)
    for g in range(DIFF_QK // DIFF_DH):
        r0 = g * DIFF_DH
        t = qt[r0:r0 + DIFF_DH, :]
        ss = jnp.sum(t * t, axis=0, keepdims=True)
        t = t * lax.rsqrt(ss * (1.0 / DIFF_DH) + EPS) * qg_ref[r0:r0 + DIFF_DH, :]
        qt_ref[0, 0, r0:r0 + DIFF_DH, :] = (t * (DIFF_DH ** -0.5 * LOG2E)).astype(BF16)
    vt_ref[0, 0] = proj_t(DIFF_QK, DIFF_WIDTH).astype(BF16)


def _in_projection(x, mod_l, norm_g, w_nat, w_t, ones_bd, qg_full, kg):
    B, S, D = x.shape
    tm = PROJ_TM
    NW = w_nat.shape[1]
    row = lambda b, i: (b, i, 0)
    tile = lambda b, i: (b, i, 0, 0)
    const = lambda b, i: (0, 0)
    widths = (1024, 512, 512, 512, GATE_PAD)
    once = pl.Buffered(1)
    return pl.pallas_call(
        _proj_kernel,
        out_shape=([jax.ShapeDtypeStruct((B, S, w), BF16) for w in widths]
                   + [jax.ShapeDtypeStruct((B, S // tm, DIFF_QK, tm), BF16),
                      jax.ShapeDtypeStruct((B, S // tm, DIFF_WIDTH, tm), BF16)]),
        grid=(B, S // tm),
        in_specs=[
            pl.BlockSpec((1, tm, D), row),
            pl.BlockSpec((1, 1, 3 * D), lambda b, i: (b, 0, 0)),
            pl.BlockSpec((1, D), const),
            pl.BlockSpec((D, NW), const, pipeline_mode=once),
            pl.BlockSpec((DIFF_QK + DIFF_WIDTH, D), const, pipeline_mode=once),
            pl.BlockSpec((512, 512), const, pipeline_mode=once),
            pl.BlockSpec((DIFF_QK, tm), const, pipeline_mode=once),
            pl.BlockSpec((1, 512), const),
        ],
        out_specs=([pl.BlockSpec((1, tm, w), row) for w in widths]
                   + [pl.BlockSpec((1, 1, DIFF_QK, tm), tile),
                      pl.BlockSpec((1, 1, DIFF_WIDTH, tm), tile)]),
        compiler_params=pltpu.CompilerParams(
            dimension_semantics=("arbitrary", "arbitrary"), vmem_limit_bytes=VMEM_LIMIT),
        name="in_proj",
    )(x, mod_l, norm_g, w_nat, w_t, ones_bd, qg_full, kg)


def _gla_kernel(u_ref, glr_ref, z_ref, cw_ref, wgk_ref, bgk_ref, tri_ref, g_ref,
                o_ref, xbuf, state):
    T = GLA_T
    t_idx = pl.program_id(1)

    @pl.when(t_idx == 0)
    def _():
        xbuf[0:8, :] = jnp.zeros((8, GLA_CONV_WIDTH), F32)
        state[...] = jnp.zeros(state.shape, F32)

    xbuf[8:8 + T, :] = u_ref[0].astype(F32)
    cw = cw_ref[...]
    conv = cw[0:1, :] * xbuf[pl.ds(8 - (CONV_K - 1), T), :]
    for j in range(1, CONV_K):
        conv = conv + cw[j:j + 1, :] * xbuf[pl.ds(8 - (CONV_K - 1) + j, T), :]
    xbuf[0:8, :] = xbuf[T:T + 8, :]
    act = _silu(conv)
    q = (act[:, 0:GLA_QK] * (GLA_DK ** -0.5)).astype(BF16)
    k = act[:, GLA_QK:2 * GLA_QK]
    v = act[:, 2 * GLA_QK:].astype(BF16)

    gate = jnp.dot(glr_ref[0], wgk_ref[...], preferred_element_type=F32) + bgk_ref[...]
    log_a = -(jnp.maximum(-gate, 0.0) + jnp.log1p(jnp.exp(-jnp.abs(gate))))
    log_a = log_a * (1.0 / GLA_GATE_TEMP)
    la_hi = log_a.astype(BF16)
    la_lo = (log_a - la_hi.astype(F32)).astype(BF16)
    tri = tri_ref[...]
    bcum = (jnp.dot(tri, la_hi, preferred_element_type=F32)
            + jnp.dot(tri, la_lo, preferred_element_type=F32))

    lane = lax.broadcasted_iota(jnp.int32, (CHUNK, LANES), 1)
    g_out = g_ref[...]
    for c in range(T // CHUNK):
        r0 = c * CHUNK
        b_c = bcum[r0:r0 + CHUNK, :]
        b_end = b_c[CHUNK - 1:CHUNK, :]
        kdec = k[r0:r0 + CHUNK, :] * jnp.exp(b_end - b_c)
        a_c = jnp.exp(b_end)
        for p in range(GLA_HEADS // 2):
            kd_pair = kdec[:, p * LANES:(p + 1) * LANES]
            q_pair = q[r0:r0 + CHUNK, p * LANES:(p + 1) * LANES]
            a_pair = a_c[:, p * LANES:(p + 1) * LANES]
            for hh in range(2):
                h = 2 * p + hh
                in_head = (lane >= hh * GLA_DK) & (lane < (hh + 1) * GLA_DK)
                kd_m = jnp.where(in_head, kd_pair, 0.0).astype(BF16)
                v_h = v[r0:r0 + CHUNK, h * GLA_DV:(h + 1) * GLA_DV]
                u_t = lax.dot_general(v_h, kd_m, (((0,), (0,)), ((), ())),
                                      preferred_element_type=F32)
                s_new = state[h] * a_pair + u_t
                state[h] = s_new
                o_h = lax.dot_general(q_pair, s_new.astype(BF16), (((1,), (1,)), ((), ())),
                                      preferred_element_type=F32)
                ms = jnp.mean(o_h * o_h, axis=-1, keepdims=True)
                zz = z_ref[0, r0:r0 + CHUNK, h * GLA_DV:(h + 1) * GLA_DV].astype(F32)
                res = o_h * lax.rsqrt(ms + EPS) * g_out * _silu(zz)
                o_ref[0, r0:r0 + CHUNK, h * GLA_DV:(h + 1) * GLA_DV] = res.astype(BF16)


def _gla_branch(gqkv, glr, gz, conv_w8, wgk_pad, bgk, tri, g_out):
    B, S, _ = gqkv.shape
    T = GLA_T
    row = lambda b, t: (b, t, 0)
    const = lambda b, t: (0, 0)
    return pl.pallas_call(
        _gla_kernel,
        out_shape=jax.ShapeDtypeStruct((B, S, GLA_WIDTH), BF16),
        grid=(B, S // T),
        in_specs=[
            pl.BlockSpec((1, T, GLA_CONV_WIDTH), row),
            pl.BlockSpec((1, T, GATE_PAD), row),
            pl.BlockSpec((1, T, GLA_WIDTH), row),
            pl.BlockSpec((8, GLA_CONV_WIDTH), const),
            pl.BlockSpec((GATE_PAD, GLA_QK), const),
            pl.BlockSpec((1, GLA_QK), const),
            pl.BlockSpec((T, T), const),
            pl.BlockSpec((1, GLA_DV), const),
        ],
        out_specs=pl.BlockSpec((1, T, GLA_WIDTH), row),
        scratch_shapes=[
            pltpu.VMEM((T + 8, GLA_CONV_WIDTH), F32),
            pltpu.VMEM((GLA_HEADS, GLA_DV, LANES), F32),
        ],
        compiler_params=pltpu.CompilerParams(
            dimension_semantics=("arbitrary", "arbitrary"), vmem_limit_bytes=VMEM_LIMIT),
        name="gla_branch",
    )(gqkv, glr, gz, conv_w8, wgk_pad, bgk, tri, g_out)


def _attn_kernel(qt_ref, k_ref, vt_ref, z_ref, lq1_ref, lk1_ref, lq2_ref, lk2_ref,
                 li_ref, g_ref, o_ref, s_a, s_b, m_s, acc_s):
    T = ATT_T
    i = pl.program_id(2)
    qt = qt_ref[0, 0]
    row = lax.broadcasted_iota(jnp.int32, (2 * DIFF_DH, T), 0)
    zero = jnp.zeros_like(qt)
    qt_maps = (jnp.where(row < DIFF_DH, qt, zero), jnp.where(row >= DIFF_DH, qt, zero))

    m_s[...] = jnp.full(m_s.shape, -jnp.inf, F32)
    acc_s[...] = jnp.zeros(acc_s.shape, F32)

    def scores(j, s_ref):
        kj = k_ref[0, pl.ds(pl.multiple_of(j * T, T), T), :]
        for mp in range(2):
            s_ref[mp, :, 0:T] = jnp.dot(kj, qt_maps[mp], preferred_element_type=F32)

    def update(j, s_ref, mask):
        vtj = jnp.concatenate([vt_ref[0, j], jnp.ones((ATT_LROWS, T), BF16)], axis=0)
        for mp in range(2):
            s = s_ref[mp, :, 0:T]
            if mask is not None:
                s = jnp.where(mask, s, -jnp.inf)
            m_old = m_s[mp]
            m_new = jnp.maximum(m_old, jnp.max(s, axis=0, keepdims=True))
            alpha = jnp.exp2(m_old - m_new)
            p = jnp.exp2(s - m_new)
            acc_s[mp] = alpha * acc_s[mp] + jnp.dot(vtj, p.astype(BF16),
                                                    preferred_element_type=F32)
            m_s[mp] = m_new

    scores(0, s_a)

    def pair(t, carry):
        j = 2 * t
        scores(j + 1, s_b)
        update(j, s_a, None)
        scores(j + 2, s_a)
        update(j + 1, s_b, None)
        return carry

    lax.fori_loop(0, i // 2, pair, 0)

    kc = lax.broadcasted_iota(jnp.int32, (T, T), 0) // CHUNK
    qc = lax.broadcasted_iota(jnp.int32, (T, T), 1) // CHUNK
    mask = kc <= qc

    @pl.when(i % 2 == 0)
    def _():
        update(i, s_a, mask)

    @pl.when(i % 2 == 1)
    def _():
        scores(i, s_b)
        update(i - 1, s_a, None)
        update(i, s_b, mask)

    lam_init = li_ref[:, 0:1]
    lam = (jnp.exp(jnp.sum(lq1_ref[...] * lk1_ref[...], axis=-1, keepdims=True))
           - jnp.exp(jnp.sum(lq2_ref[...] * lk2_ref[...], axis=-1, keepdims=True))
           + lam_init)
    acc0, acc1 = acc_s[0], acc_s[1]
    out = (acc0[0:DIFF_DV] / acc0[DIFF_DV:DIFF_DV + 1]
           - lam * (acc1[0:DIFF_DV] / acc1[DIFF_DV:DIFF_DV + 1]))
    ms = jnp.mean(out * out, axis=0, keepdims=True)
    out = out * lax.rsqrt(ms + EPS) * g_ref[...] * (1.0 - lam_init)
    o_ref[0] = (out.T * _silu(z_ref[0].astype(F32))).astype(BF16)


def _diff_attention(qt, dk, vt, dz, lq1, lk1, lq2, lk2, lam_init_row, g_full):
    B, S, _ = dk.shape
    T = ATT_T
    H = DIFF_HEADS
    nt = S // T
    qmap = lambda b, h, i: (b, i, h)
    const = lambda b, h, i: (0, 0)
    return pl.pallas_call(
        _attn_kernel,
        out_shape=jax.ShapeDtypeStruct((B, S, DIFF_WIDTH), BF16),
        grid=(B, H, nt),
        in_specs=[
            pl.BlockSpec((1, 1, 2 * DIFF_DH, T), lambda b, h, i: (b, i, h, 0)),
            pl.BlockSpec((1, S, LANES), lambda b, h, i: (b, 0, h)),
            pl.BlockSpec((1, nt, DIFF_DV, T), lambda b, h, i: (b, 0, h, 0)),
            pl.BlockSpec((1, T, LANES), qmap),
            pl.BlockSpec((1, DIFF_DH), const),
            pl.BlockSpec((1, DIFF_DH), const),
            pl.BlockSpec((1, DIFF_DH), const),
            pl.BlockSpec((1, DIFF_DH), const),
            pl.BlockSpec((1, LANES), const),
            pl.BlockSpec((DIFF_DV, T), const),
        ],
        out_specs=pl.BlockSpec((1, T, LANES), qmap),
        scratch_shapes=[
            pltpu.VMEM((2, T, T + ATT_SPAD), F32),
            pltpu.VMEM((2, T, T + ATT_SPAD), F32),
            pltpu.VMEM((2, 1, T), F32),
            pltpu.VMEM((2, DIFF_DV + ATT_LROWS, T), F32),
        ],
        compiler_params=pltpu.CompilerParams(
            dimension_semantics=("arbitrary", "arbitrary", "arbitrary"),
            vmem_limit_bytes=VMEM_LIMIT),
        name="diff_attn",
    )(qt, dk, vt, dz, lq1, lk1, lq2, lk2, lam_init_row, g_full)


def _out_kernel(og_ref, od_ref, x_ref, mod_ref, w_ref, o_ref):
    D = D_MODEL
    gate = mod_ref[0][:, 2 * D:3 * D]
    y = (jnp.dot(og_ref[0], w_ref[0:GLA_WIDTH, :], preferred_element_type=F32)
         + jnp.dot(od_ref[0], w_ref[GLA_WIDTH:MIX_WIDTH, :], preferred_element_type=F32))
    o_ref[0] = x_ref[0] + gate * y


def _out_projection(o_gla, o_diff, x, mod_l, w_out):
    B, S, D = x.shape
    tm = OUT_TM
    row = lambda b, i: (b, i, 0)
    return pl.pallas_call(
        _out_kernel,
        out_shape=jax.ShapeDtypeStruct((B, S, D), F32),
        grid=(B, S // tm),
        in_specs=[
            pl.BlockSpec((1, tm, GLA_WIDTH), row),
            pl.BlockSpec((1, tm, DIFF_WIDTH), row),
            pl.BlockSpec((1, tm, D), row),
            pl.BlockSpec((1, 1, 3 * D), lambda b, i: (b, 0, 0)),
            pl.BlockSpec((MIX_WIDTH, D), lambda b, i: (0, 0), pipeline_mode=pl.Buffered(1)),
        ],
        out_specs=pl.BlockSpec((1, tm, D), row),
        compiler_params=pltpu.CompilerParams(
            dimension_semantics=("arbitrary", "arbitrary"), vmem_limit_bytes=VMEM_LIMIT),
        name="out_proj",
    )(o_gla, o_diff, x, mod_l, w_out)


def _split_w_in(w):
    idx = [0]
    for s in SPLIT_SIZES:
        idx.append(idx[-1] + s)
    gq, gk, gv, glr, gz, dq, dk, dv, dz = [w[:, idx[n]:idx[n + 1]] for n in range(9)]
    glr = jnp.pad(glr, ((0, 0), (0, GATE_PAD - GLA_GATE_RANK)))
    w_nat = jnp.concatenate([gq, gk, gv, gz, dk, dz, glr], axis=1).astype(BF16)
    w_t = jnp.concatenate([dq, dv], axis=1).T.astype(BF16)
    return w_nat, w_t


def kernel(x, c, w_ada, b_ada, norm_g, w_in, conv_w, w_gk, b_gk, gla_norm_g,
           qn_g, kn_g, lam_q1, lam_k1, lam_q2, lam_k2, diff_norm_g, w_out):
    B, S, D = x.shape
    mod = _modulation(c, w_ada, b_ada)

    r = jnp.arange(DIFF_QK) // DIFF_DH
    ones_bd = (r[:, None] == r[None, :]).astype(BF16)
    t = jnp.arange(GLA_T)
    tri = ((t[:, None] // CHUNK == t[None, :] // CHUNK)
           & (t[None, :] <= t[:, None])).astype(BF16)

    for l in range(DEPTH):
        mod_l = mod[l].reshape(B, 1, 3 * D)
        w_nat, w_t = _split_w_in(w_in[l])
        qg_full = jnp.broadcast_to(
            jnp.tile(qn_g[l], DIFF_QK // DIFF_DH)[:, None], (DIFF_QK, PROJ_TM))
        kg = jnp.tile(kn_g[l], DIFF_QK // DIFF_DH).reshape(1, DIFF_QK)
        gqkv, gz, dk, dz, glr, qt, vt = _in_projection(
            x, mod_l, norm_g[l].reshape(1, D), w_nat, w_t, ones_bd, qg_full, kg)

        conv_w8 = jnp.pad(conv_w[l], ((0, 8 - CONV_K), (0, 0)))
        wgk_pad = jnp.pad(w_gk[l], ((0, GATE_PAD - GLA_GATE_RANK), (0, 0))).astype(BF16)
        o_gla = _gla_branch(gqkv, glr, gz, conv_w8, wgk_pad, b_gk[l].reshape(1, GLA_QK),
                            tri, gla_norm_g[l].reshape(1, GLA_DV))

        lam_init = 0.8 - 0.6 * math.exp(-0.3 * l)
        o_diff = _diff_attention(
            qt, dk, vt, dz,
            lam_q1[l].reshape(1, DIFF_DH), lam_k1[l].reshape(1, DIFF_DH),
            lam_q2[l].reshape(1, DIFF_DH), lam_k2[l].reshape(1, DIFF_DH),
            jnp.full((1, LANES), lam_init, F32),
            jnp.broadcast_to(diff_norm_g[l][:, None], (DIFF_DV, ATT_T)))

        x = _out_projection(o_gla, o_diff, x, mod_l, w_out[l].astype(BF16))
    return x
```

```python
import functools
import math

import jax
import jax.numpy as jnp
from jax import lax
from jax.experimental import pallas as pl
from jax.experimental.pallas import tpu as pltpu

D_MODEL = 1024
DEPTH = 4
CHUNK = 64
CONV_K = 4
EPS = 1e-6

GLA_HEADS = 4
GLA_DK = 64
GLA_DV = 128
GLA_QK = GLA_HEADS * GLA_DK
GLA_WIDTH = GLA_HEADS * GLA_DV
GLA_GATE_RANK = 16
GLA_GATE_TEMP = 16.0

DIFF_HEADS = 4
DIFF_DH = 64
DIFF_DV = 2 * DIFF_DH
DIFF_QK = DIFF_HEADS * 2 * DIFF_DH
DIFF_WIDTH = DIFF_HEADS * DIFF_DV

MIX_WIDTH = GLA_WIDTH + DIFF_WIDTH
GLA_CONV_WIDTH = 2 * GLA_QK + GLA_WIDTH
SPLIT_SIZES = (GLA_QK, GLA_QK, GLA_WIDTH, GLA_GATE_RANK, GLA_WIDTH,
               DIFF_QK, DIFF_QK, DIFF_WIDTH, DIFF_WIDTH)

LANES = 128
GATE_PAD = LANES
VMEM_LIMIT = 56 * 1024 * 1024

ATT_T = 512
ATT_LROWS = 16
PROJ_TM = ATT_T
OUT_TM = 512
GLA_T = 512
MOD_TN = 512

F32 = jnp.float32
BF16 = jnp.bfloat16
LOG2E = math.log2(math.e)


def _sigmoid(x):
    return 1.0 / (1.0 + jnp.exp(-x))


def _silu(x):
    return x * _sigmoid(x)


def _mod_kernel(ct_ref, w_ref, b_ref, o_ref):
    ct = ct_ref[...]
    ca = _silu(ct)
    w = w_ref[0]
    rows = []
    for b in range(ct.shape[1]):
        rows.append(jnp.sum(w * ca[:, b:b + 1], axis=0, keepdims=True))
    o_ref[0] = jnp.concatenate(rows, axis=0) + b_ref[0]


def _modulation(c, w_ada, b_ada):
    B, D = c.shape
    L, _, N = w_ada.shape
    return pl.pallas_call(
        _mod_kernel,
        out_shape=jax.ShapeDtypeStruct((L, B, N), F32),
        grid=(L, N // MOD_TN),
        in_specs=[
            pl.BlockSpec((D, B), lambda l, n: (0, 0)),
            pl.BlockSpec((1, D, MOD_TN), lambda l, n: (l, 0, n)),
            pl.BlockSpec((1, 1, MOD_TN), lambda l, n: (l, 0, n)),
        ],
        out_specs=pl.BlockSpec((1, B, MOD_TN), lambda l, n: (l, 0, n)),
        compiler_params=pltpu.CompilerParams(
            dimension_semantics=("arbitrary", "arbitrary"), vmem_limit_bytes=VMEM_LIMIT),
        name="adaln_mod",
    )(c.T, w_ada, b_ada.reshape(L, 1, N))


def _group_rms(t, ones_bd, gain):
    ss = jnp.dot((t * t).astype(BF16), ones_bd, preferred_element_type=F32)
    return t * lax.rsqrt(ss * (1.0 / DIFF_DH) + EPS) * gain


def _proj_kernel(x_ref, mod_ref, g_ref, w_ref, wt_ref, ones_ref, qg_ref, kg_ref,
                 gqkv_ref, gz_ref, dk_ref, dz_ref, glr_ref, qt_ref, vt_ref):
    D = D_MODEL
    x = x_ref[0]
    mod = mod_ref[0]
    shift = mod[:, 0:D]
    scale = mod[:, D:2 * D]
    ms = jnp.mean(x * x, axis=-1, keepdims=True)
    y = x * lax.rsqrt(ms + EPS) * g_ref[...]
    h = (y * (1.0 + scale) + shift).astype(BF16)

    def proj(a, n):
        return jnp.dot(h, w_ref[:, a:a + n], preferred_element_type=F32)

    def proj_t(a, n):
        return lax.dot_general(wt_ref[a:a + n, :], h, (((1,), (1,)), ((), ())),
                               preferred_element_type=F32)

    gqkv_ref[0, :, 0:512] = proj(0, 512).astype(BF16)
    gqkv_ref[0, :, 512:1024] = proj(512, 512).astype(BF16)
    gz_ref[0] = proj(1024, 512).astype(BF16)
    dk_ref[0] = _group_rms(proj(1536, 512), ones_ref[...], kg_ref[...]).astype(BF16)
    dz_ref[0] = proj(2048, 512).astype(BF16)
    glr_ref[0] = proj(2560, GATE_PAD).astype(BF16)

    qt = proj_t(0, DIFF_QK)
    for g in range(DIFF_QK // DIFF_DH):
        r0 = g * DIFF_DH
        t = qt[r0:r0 + DIFF_DH, :]
        ss = jnp.sum(t * t, axis=0, keepdims=True)
        t = t * lax.rsqrt(ss * (1.0 / DIFF_DH) + EPS) * qg_ref[r0:r0 + DIFF_DH, :]
        qt_ref[0, 0, r0:r0 + DIFF_DH, :] = (t * (DIFF_DH ** -0.5 * LOG2E)).astype(BF16)
    vt_ref[0, 0] = proj_t(DIFF_QK, DIFF_WIDTH).astype(BF16)


def _in_projection(x, mod_l, norm_g, w_nat, w_t, ones_bd, qg_full, kg):
    B, S, D = x.shape
    tm = PROJ_TM
    NW = w_nat.shape[1]
    row = lambda b, i: (b, i, 0)
    tile = lambda b, i: (b, i, 0, 0)
    const = lambda b, i: (0, 0)
    widths = (1024, 512, 512, 512, GATE_PAD)
    once = pl.Buffered(1)
    return pl.pallas_call(
        _proj_kernel,
        out_shape=([jax.ShapeDtypeStruct((B, S, w), BF16) for w in widths]
                   + [jax.ShapeDtypeStruct((B, S // tm, DIFF_QK, tm), BF16),
                      jax.ShapeDtypeStruct((B, S // tm, DIFF_WIDTH, tm), BF16)]),
        grid=(B, S // tm),
        in_specs=[
            pl.BlockSpec((1, tm, D), row),
            pl.BlockSpec((1, 1, 3 * D), lambda b, i: (b, 0, 0)),
            pl.BlockSpec((1, D), const),
            pl.BlockSpec((D, NW), const, pipeline_mode=once),
            pl.BlockSpec((DIFF_QK + DIFF_WIDTH, D), const, pipeline_mode=once),
            pl.BlockSpec((512, 512), const, pipeline_mode=once),
            pl.BlockSpec((DIFF_QK, tm), const, pipeline_mode=once),
            pl.BlockSpec((1, 512), const),
        ],
        out_specs=([pl.BlockSpec((1, tm, w), row) for w in widths]
                   + [pl.BlockSpec((1, 1, DIFF_QK, tm), tile),
                      pl.BlockSpec((1, 1, DIFF_WIDTH, tm), tile)]),
        compiler_params=pltpu.CompilerParams(
            dimension_semantics=("arbitrary", "arbitrary"), vmem_limit_bytes=VMEM_LIMIT),
        name="in_proj",
    )(x, mod_l, norm_g, w_nat, w_t, ones_bd, qg_full, kg)


def _gla_kernel(u_ref, glr_ref, z_ref, cw_ref, wgk_ref, bgk_ref, tri_ref, g_ref,
                o_ref, xbuf, state):
    T = GLA_T
    t_idx = pl.program_id(1)

    @pl.when(t_idx == 0)
    def _():
        xbuf[0:8, :] = jnp.zeros((8, GLA_CONV_WIDTH), F32)
        state[...] = jnp.zeros(state.shape, F32)

    xbuf[8:8 + T, :] = u_ref[0].astype(F32)
    cw = cw_ref[...]
    conv = cw[0:1, :] * xbuf[pl.ds(8 - (CONV_K - 1), T), :]
    for j in range(1, CONV_K):
        conv = conv + cw[j:j + 1, :] * xbuf[pl.ds(8 - (CONV_K - 1) + j, T), :]
    xbuf[0:8, :] = xbuf[T:T + 8, :]
    act = _silu(conv)
    q = (act[:, 0:GLA_QK] * (GLA_DK ** -0.5)).astype(BF16)
    k = act[:, GLA_QK:2 * GLA_QK]
    v = act[:, 2 * GLA_QK:].astype(BF16)

    gate = jnp.dot(glr_ref[0], wgk_ref[...], preferred_element_type=F32) + bgk_ref[...]
    log_a = -(jnp.maximum(-gate, 0.0) + jnp.log1p(jnp.exp(-jnp.abs(gate))))
    log_a = log_a * (1.0 / GLA_GATE_TEMP)
    la_hi = log_a.astype(BF16)
    la_lo = (log_a - la_hi.astype(F32)).astype(BF16)
    tri = tri_ref[...]
    bcum = (jnp.dot(tri, la_hi, preferred_element_type=F32)
            + jnp.dot(tri, la_lo, preferred_element_type=F32))

    lane = lax.broadcasted_iota(jnp.int32, (CHUNK, LANES), 1)
    g_out = g_ref[...]
    for c in range(T // CHUNK):
        r0 = c * CHUNK
        b_c = bcum[r0:r0 + CHUNK, :]
        b_end = b_c[CHUNK - 1:CHUNK, :]
        kdec = k[r0:r0 + CHUNK, :] * jnp.exp(b_end - b_c)
        a_c = jnp.exp(b_end)
        for p in range(GLA_HEADS // 2):
            kd_pair = kdec[:, p * LANES:(p + 1) * LANES]
            q_pair = q[r0:r0 + CHUNK, p * LANES:(p + 1) * LANES]
            a_pair = a_c[:, p * LANES:(p + 1) * LANES]
            for hh in range(2):
                h = 2 * p + hh
                in_head = (lane >= hh * GLA_DK) & (lane < (hh + 1) * GLA_DK)
                kd_m = jnp.where(in_head, kd_pair, 0.0).astype(BF16)
                v_h = v[r0:r0 + CHUNK, h * GLA_DV:(h + 1) * GLA_DV]
                u_t = lax.dot_general(v_h, kd_m, (((0,), (0,)), ((), ())),
                                      preferred_element_type=F32)
                s_new = state[h] * a_pair + u_t
                state[h] = s_new
                o_h = lax.dot_general(q_pair, s_new.astype(BF16), (((1,), (1,)), ((), ())),
                                      preferred_element_type=F32)
                ms = jnp.mean(o_h * o_h, axis=-1, keepdims=True)
                zz = z_ref[0, r0:r0 + CHUNK, h * GLA_DV:(h + 1) * GLA_DV].astype(F32)
                res = o_h * lax.rsqrt(ms + EPS) * g_out * _silu(zz)
                o_ref[0, r0:r0 + CHUNK, h * GLA_DV:(h + 1) * GLA_DV] = res.astype(BF16)


def _gla_branch(gqkv, glr, gz, conv_w8, wgk_pad, bgk, tri, g_out):
    B, S, _ = gqkv.shape
    T = GLA_T
    row = lambda b, t: (b, t, 0)
    const = lambda b, t: (0, 0)
    return pl.pallas_call(
        _gla_kernel,
        out_shape=jax.ShapeDtypeStruct((B, S, GLA_WIDTH), BF16),
        grid=(B, S // T),
        in_specs=[
            pl.BlockSpec((1, T, GLA_CONV_WIDTH), row),
            pl.BlockSpec((1, T, GATE_PAD), row),
            pl.BlockSpec((1, T, GLA_WIDTH), row),
            pl.BlockSpec((8, GLA_CONV_WIDTH), const),
            pl.BlockSpec((GATE_PAD, GLA_QK), const),
            pl.BlockSpec((1, GLA_QK), const),
            pl.BlockSpec((T, T), const),
            pl.BlockSpec((1, GLA_DV), const),
        ],
        out_specs=pl.BlockSpec((1, T, GLA_WIDTH), row),
        scratch_shapes=[
            pltpu.VMEM((T + 8, GLA_CONV_WIDTH), F32),
            pltpu.VMEM((GLA_HEADS, GLA_DV, LANES), F32),
        ],
        compiler_params=pltpu.CompilerParams(
            dimension_semantics=("arbitrary", "arbitrary"), vmem_limit_bytes=VMEM_LIMIT),
        name="gla_branch",
    )(gqkv, glr, gz, conv_w8, wgk_pad, bgk, tri, g_out)


def _attn_kernel(qt_ref, k_ref, vt_ref, z_ref, lq1_ref, lk1_ref, lq2_ref, lk2_ref,
                 li_ref, g_ref, o_ref, s_a, s_b, cm_a, cm_b, m_s, acc_s):
    T = ATT_T
    i = pl.program_id(2)
    qt = qt_ref[0, 0]
    row = lax.broadcasted_iota(jnp.int32, (2 * DIFF_DH, T), 0)
    zero = jnp.zeros_like(qt)
    qt_maps = (jnp.where(row < DIFF_DH, qt, zero), jnp.where(row >= DIFF_DH, qt, zero))

    m_s[...] = jnp.full(m_s.shape, -jnp.inf, F32)
    acc_s[...] = jnp.zeros(acc_s.shape, F32)

    def scores(j, s_ref, cm_ref):
        kj = k_ref[0, pl.ds(pl.multiple_of(j * T, T), T), :]
        for mp in range(2):
            s = jnp.dot(kj, qt_maps[mp], preferred_element_type=F32)
            s_ref[mp] = s
            cm_ref[mp] = jnp.max(s, axis=0, keepdims=True)

    def update(j, s_ref, cm_ref, mask):
        vtj = jnp.concatenate([vt_ref[0, j], jnp.ones((ATT_LROWS, T), BF16)], axis=0)
        for mp in range(2):
            s = s_ref[mp]
            if mask is None:
                tile_max = cm_ref[mp]
            else:
                s = jnp.where(mask, s, -jnp.inf)
                tile_max = jnp.max(s, axis=0, keepdims=True)
            m_old = m_s[mp]
            m_new = jnp.maximum(m_old, tile_max)
            alpha = jnp.exp2(m_old - m_new)
            p = jnp.exp2(s - m_new)
            acc_s[mp] = alpha * acc_s[mp] + jnp.dot(vtj, p.astype(BF16),
                                                    preferred_element_type=F32)
            m_s[mp] = m_new

    scores(0, s_a, cm_a)

    def pair(t, carry):
        j = 2 * t
        scores(j + 1, s_b, cm_b)
        update(j, s_a, cm_a, None)
        scores(j + 2, s_a, cm_a)
        update(j + 1, s_b, cm_b, None)
        return carry

    lax.fori_loop(0, i // 2, pair, 0)

    kc = lax.broadcasted_iota(jnp.int32, (T, T), 0) // CHUNK
    qc = lax.broadcasted_iota(jnp.int32, (T, T), 1) // CHUNK
    mask = kc <= qc

    @pl.when(i % 2 == 0)
    def _():
        update(i, s_a, cm_a, mask)

    @pl.when(i % 2 == 1)
    def _():
        scores(i, s_b, cm_b)
        update(i - 1, s_a, cm_a, None)
        update(i, s_b, cm_b, mask)

    lam_init = li_ref[:, 0:1]
    lam = (jnp.exp(jnp.sum(lq1_ref[...] * lk1_ref[...], axis=-1, keepdims=True))
           - jnp.exp(jnp.sum(lq2_ref[...] * lk2_ref[...], axis=-1, keepdims=True))
           + lam_init)
    acc0, acc1 = acc_s[0], acc_s[1]
    out = (acc0[0:DIFF_DV] / acc0[DIFF_DV:DIFF_DV + 1]
           - lam * (acc1[0:DIFF_DV] / acc1[DIFF_DV:DIFF_DV + 1]))
    ms = jnp.mean(out * out, axis=0, keepdims=True)
    out = out * lax.rsqrt(ms + EPS) * g_ref[...] * (1.0 - lam_init)
    o_ref[0] = (out.T * _silu(z_ref[0].astype(F32))).astype(BF16)


def _diff_attention(qt, dk, vt, dz, lq1, lk1, lq2, lk2, lam_init_row, g_full):
    B, S, _ = dk.shape
    T = ATT_T
    H = DIFF_HEADS
    nt = S // T
    qmap = lambda b, h, i: (b, i, h)
    const = lambda b, h, i: (0, 0)
    return pl.pallas_call(
        _attn_kernel,
        out_shape=jax.ShapeDtypeStruct((B, S, DIFF_WIDTH), BF16),
        grid=(B, H, nt),
        in_specs=[
            pl.BlockSpec((1, 1, 2 * DIFF_DH, T), lambda b, h, i: (b, i, h, 0)),
            pl.BlockSpec((1, S, LANES), lambda b, h, i: (b, 0, h)),
            pl.BlockSpec((1, nt, DIFF_DV, T), lambda b, h, i: (b, 0, h, 0)),
            pl.BlockSpec((1, T, LANES), qmap),
            pl.BlockSpec((1, DIFF_DH), const),
            pl.BlockSpec((1, DIFF_DH), const),
            pl.BlockSpec((1, DIFF_DH), const),
            pl.BlockSpec((1, DIFF_DH), const),
            pl.BlockSpec((1, LANES), const),
            pl.BlockSpec((DIFF_DV, T), const),
        ],
        out_specs=pl.BlockSpec((1, T, LANES), qmap),
        scratch_shapes=[
            pltpu.VMEM((2, T, T), F32),
            pltpu.VMEM((2, T, T), F32),
            pltpu.VMEM((2, 1, T), F32),
            pltpu.VMEM((2, 1, T), F32),
            pltpu.VMEM((2, 1, T), F32),
            pltpu.VMEM((2, DIFF_DV + ATT_LROWS, T), F32),
        ],
        compiler_params=pltpu.CompilerParams(
            dimension_semantics=("arbitrary", "arbitrary", "arbitrary"),
            vmem_limit_bytes=VMEM_LIMIT),
        name="diff_attn",
    )(qt, dk, vt, dz, lq1, lk1, lq2, lk2, lam_init_row, g_full)


def _out_kernel(og_ref, od_ref, x_ref, mod_ref, w_ref, o_ref):
    D = D_MODEL
    gate = mod_ref[0][:, 2 * D:3 * D]
    y = (jnp.dot(og_ref[0], w_ref[0:GLA_WIDTH, :], preferred_element_type=F32)
         + jnp.dot(od_ref[0], w_ref[GLA_WIDTH:MIX_WIDTH, :], preferred_element_type=F32))
    o_ref[0] = x_ref[0] + gate * y


def _out_projection(o_gla, o_diff, x, mod_l, w_out):
    B, S, D = x.shape
    tm = OUT_TM
    row = lambda b, i: (b, i, 0)
    return pl.pallas_call(
        _out_kernel,
        out_shape=jax.ShapeDtypeStruct((B, S, D), F32),
        grid=(B, S // tm),
        in_specs=[
            pl.BlockSpec((1, tm, GLA_WIDTH), row),
            pl.BlockSpec((1, tm, DIFF_WIDTH), row),
            pl.BlockSpec((1, tm, D), row),
            pl.BlockSpec((1, 1, 3 * D), lambda b, i: (b, 0, 0)),
            pl.BlockSpec((MIX_WIDTH, D), lambda b, i: (0, 0), pipeline_mode=pl.Buffered(1)),
        ],
        out_specs=pl.BlockSpec((1, tm, D), row),
        compiler_params=pltpu.CompilerParams(
            dimension_semantics=("arbitrary", "arbitrary"), vmem_limit_bytes=VMEM_LIMIT),
        name="out_proj",
    )(o_gla, o_diff, x, mod_l, w_out)


def _split_w_in(w):
    idx = [0]
    for s in SPLIT_SIZES:
        idx.append(idx[-1] + s)
    gq, gk, gv, glr, gz, dq, dk, dv, dz = [w[:, idx[n]:idx[n + 1]] for n in range(9)]
    glr = jnp.pad(glr, ((0, 0), (0, GATE_PAD - GLA_GATE_RANK)))
    w_nat = jnp.concatenate([gq, gk, gv, gz, dk, dz, glr], axis=1).astype(BF16)
    w_t = jnp.concatenate([dq, dv], axis=1).T.astype(BF16)
    return w_nat, w_t


def kernel(x, c, w_ada, b_ada, norm_g, w_in, conv_w, w_gk, b_gk, gla_norm_g,
           qn_g, kn_g, lam_q1, lam_k1, lam_q2, lam_k2, diff_norm_g, w_out):
    B, S, D = x.shape
    mod = _modulation(c, w_ada, b_ada)

    r = jnp.arange(DIFF_QK) // DIFF_DH
    ones_bd = (r[:, None] == r[None, :]).astype(BF16)
    t = jnp.arange(GLA_T)
    tri = ((t[:, None] // CHUNK == t[None, :] // CHUNK)
           & (t[None, :] <= t[:, None])).astype(BF16)

    for l in range(DEPTH):
        mod_l = mod[l].reshape(B, 1, 3 * D)
        w_nat, w_t = _split_w_in(w_in[l])
        qg_full = jnp.broadcast_to(
            jnp.tile(qn_g[l], DIFF_QK // DIFF_DH)[:, None], (DIFF_QK, PROJ_TM))
        kg = jnp.tile(kn_g[l], DIFF_QK // DIFF_DH).reshape(1, DIFF_QK)
        gqkv, gz, dk, dz, glr, qt, vt = _in_projection(
            x, mod_l, norm_g[l].reshape(1, D), w_nat, w_t, ones_bd, qg_full, kg)

        conv_w8 = jnp.pad(conv_w[l], ((0, 8 - CONV_K), (0, 0)))
        wgk_pad = jnp.pad(w_gk[l], ((0, GATE_PAD - GLA_GATE_RANK), (0, 0))).astype(BF16)
        o_gla = _gla_branch(gqkv, glr, gz, conv_w8, wgk_pad, b_gk[l].reshape(1, GLA_QK),
                            tri, gla_norm_g[l].reshape(1, GLA_DV))

        lam_init = 0.8 - 0.6 * math.exp(-0.3 * l)
        o_diff = _diff_attention(
            qt, dk, vt, dz,
            lam_q1[l].reshape(1, DIFF_DH), lam_k1[l].reshape(1, DIFF_DH),
            lam_q2[l].reshape(1, DIFF_DH), lam_k2[l].reshape(1, DIFF_DH),
            jnp.full((1, LANES), lam_init, F32),
            jnp.broadcast_to(diff_norm_g[l][:, None], (DIFF_DV, ATT_T)))

        x = _out_projection(o_gla, o_diff, x, mod_l, w_out[l].astype(BF16))
    return x
```

```python
import functools
import math

import jax
import jax.numpy as jnp
from jax import lax
from jax.experimental import pallas as pl
from jax.experimental.pallas import tpu as pltpu

D_MODEL = 1024
DEPTH = 4
CHUNK = 64
CONV_K = 4
EPS = 1e-6

GLA_HEADS = 4
GLA_DK = 64
GLA_DV = 128
GLA_QK = GLA_HEADS * GLA_DK
GLA_WIDTH = GLA_HEADS * GLA_DV
GLA_GATE_RANK = 16
GLA_GATE_TEMP = 16.0

DIFF_HEADS = 4
DIFF_DH = 64
DIFF_DV = 2 * DIFF_DH
DIFF_QK = DIFF_HEADS * 2 * DIFF_DH
DIFF_WIDTH = DIFF_HEADS * DIFF_DV

MIX_WIDTH = GLA_WIDTH + DIFF_WIDTH
GLA_CONV_WIDTH = 2 * GLA_QK + GLA_WIDTH
SPLIT_SIZES = (GLA_QK, GLA_QK, GLA_WIDTH, GLA_GATE_RANK, GLA_WIDTH,
               DIFF_QK, DIFF_QK, DIFF_WIDTH, DIFF_WIDTH)

LANES = 128
GATE_PAD = LANES
VMEM_LIMIT = 56 * 1024 * 1024

ATT_T = 512
ATT_UNROLL = 2
ATT_SCORE_BOUND = 32.0
ATT_LROWS = 16
PROJ_TM = ATT_T
OUT_TM = 512
GLA_T = 512
MOD_TN = 512

F32 = jnp.float32
BF16 = jnp.bfloat16
LOG2E = math.log2(math.e)


def _sigmoid(x):
    return 1.0 / (1.0 + jnp.exp(-x))


def _silu(x):
    return x * _sigmoid(x)


def _mod_kernel(ct_ref, w_ref, b_ref, o_ref):
    ct = ct_ref[...]
    ca = _silu(ct)
    w = w_ref[0]
    rows = []
    for b in range(ct.shape[1]):
        rows.append(jnp.sum(w * ca[:, b:b + 1], axis=0, keepdims=True))
    o_ref[0] = jnp.concatenate(rows, axis=0) + b_ref[0]


def _modulation(c, w_ada, b_ada):
    B, D = c.shape
    L, _, N = w_ada.shape
    return pl.pallas_call(
        _mod_kernel,
        out_shape=jax.ShapeDtypeStruct((L, B, N), F32),
        grid=(L, N // MOD_TN),
        in_specs=[
            pl.BlockSpec((D, B), lambda l, n: (0, 0)),
            pl.BlockSpec((1, D, MOD_TN), lambda l, n: (l, 0, n)),
            pl.BlockSpec((1, 1, MOD_TN), lambda l, n: (l, 0, n)),
        ],
        out_specs=pl.BlockSpec((1, B, MOD_TN), lambda l, n: (l, 0, n)),
        compiler_params=pltpu.CompilerParams(
            dimension_semantics=("arbitrary", "arbitrary"), vmem_limit_bytes=VMEM_LIMIT),
        name="adaln_mod",
    )(c.T, w_ada, b_ada.reshape(L, 1, N))


def _group_rms(t, ones_bd, gain):
    ss = jnp.dot((t * t).astype(BF16), ones_bd, preferred_element_type=F32)
    return t * lax.rsqrt(ss * (1.0 / DIFF_DH) + EPS) * gain


def _proj_kernel(x_ref, mod_ref, g_ref, w_ref, wt_ref, ones_ref, qg_ref, kg_ref,
                 gqkv_ref, gz_ref, dk_ref, dz_ref, glr_ref, qt_ref, vt_ref):
    D = D_MODEL
    x = x_ref[0]
    mod = mod_ref[0]
    shift = mod[:, 0:D]
    scale = mod[:, D:2 * D]
    ms = jnp.mean(x * x, axis=-1, keepdims=True)
    y = x * lax.rsqrt(ms + EPS) * g_ref[...]
    h = (y * (1.0 + scale) + shift).astype(BF16)

    def proj(a, n):
        return jnp.dot(h, w_ref[:, a:a + n], preferred_element_type=F32)

    def proj_t(a, n):
        return lax.dot_general(wt_ref[a:a + n, :], h, (((1,), (1,)), ((), ())),
                               preferred_element_type=F32)

    gqkv_ref[0, :, 0:512] = proj(0, 512).astype(BF16)
    gqkv_ref[0, :, 512:1024] = proj(512, 512).astype(BF16)
    gz_ref[0] = proj(1024, 512).astype(BF16)
    dk_ref[0] = _group_rms(proj(1536, 512), ones_ref[...], kg_ref[...]).astype(BF16)
    dz_ref[0] = proj(2048, 512).astype(BF16)
    glr_ref[0] = proj(2560, GATE_PAD).astype(BF16)

    qt = proj_t(0, DIFF_QK)
    for g in range(DIFF_QK // DIFF_DH):
        r0 = g * DIFF_DH
        t = qt[r0:r0 + DIFF_DH, :]
        ss = jnp.sum(t * t, axis=0, keepdims=True)
        t = t * lax.rsqrt(ss * (1.0 / DIFF_DH) + EPS) * qg_ref[r0:r0 + DIFF_DH, :]
        qt_ref[0, 0, r0:r0 + DIFF_DH, :] = (t * (DIFF_DH ** -0.5 * LOG2E)).astype(BF16)
    vt_ref[0, 0] = proj_t(DIFF_QK, DIFF_WIDTH).astype(BF16)


def _in_projection(x, mod_l, norm_g, w_nat, w_t, ones_bd, qg_full, kg):
    B, S, D = x.shape
    tm = PROJ_TM
    NW = w_nat.shape[1]
    row = lambda b, i: (b, i, 0)
    tile = lambda b, i: (b, i, 0, 0)
    const = lambda b, i: (0, 0)
    widths = (1024, 512, 512, 512, GATE_PAD)
    once = pl.Buffered(1)
    return pl.pallas_call(
        _proj_kernel,
        out_shape=([jax.ShapeDtypeStruct((B, S, w), BF16) for w in widths]
                   + [jax.ShapeDtypeStruct((B, S // tm, DIFF_QK, tm), BF16),
                      jax.ShapeDtypeStruct((B, S // tm, DIFF_WIDTH, tm), BF16)]),
        grid=(B, S // tm),
        in_specs=[
            pl.BlockSpec((1, tm, D), row),
            pl.BlockSpec((1, 1, 3 * D), lambda b, i: (b, 0, 0)),
            pl.BlockSpec((1, D), const),
            pl.BlockSpec((D, NW), const, pipeline_mode=once),
            pl.BlockSpec((DIFF_QK + DIFF_WIDTH, D), const, pipeline_mode=once),
            pl.BlockSpec((512, 512), const, pipeline_mode=once),
            pl.BlockSpec((DIFF_QK, tm), const, pipeline_mode=once),
            pl.BlockSpec((1, 512), const),
        ],
        out_specs=([pl.BlockSpec((1, tm, w), row) for w in widths]
                   + [pl.BlockSpec((1, 1, DIFF_QK, tm), tile),
                      pl.BlockSpec((1, 1, DIFF_WIDTH, tm), tile)]),
        compiler_params=pltpu.CompilerParams(
            dimension_semantics=("arbitrary", "arbitrary"), vmem_limit_bytes=VMEM_LIMIT),
        name="in_proj",
    )(x, mod_l, norm_g, w_nat, w_t, ones_bd, qg_full, kg)


def _gla_kernel(u_ref, glr_ref, z_ref, cw_ref, wgk_ref, bgk_ref, tri_ref, g_ref,
                o_ref, xbuf, state):
    T = GLA_T
    t_idx = pl.program_id(1)

    @pl.when(t_idx == 0)
    def _():
        xbuf[0:8, :] = jnp.zeros((8, GLA_CONV_WIDTH), F32)
        state[...] = jnp.zeros(state.shape, F32)

    xbuf[8:8 + T, :] = u_ref[0].astype(F32)
    cw = cw_ref[...]
    conv = cw[0:1, :] * xbuf[pl.ds(8 - (CONV_K - 1), T), :]
    for j in range(1, CONV_K):
        conv = conv + cw[j:j + 1, :] * xbuf[pl.ds(8 - (CONV_K - 1) + j, T), :]
    xbuf[0:8, :] = xbuf[T:T + 8, :]
    act = _silu(conv)
    q = (act[:, 0:GLA_QK] * (GLA_DK ** -0.5)).astype(BF16)
    k = act[:, GLA_QK:2 * GLA_QK]
    v = act[:, 2 * GLA_QK:].astype(BF16)

    gate = jnp.dot(glr_ref[0], wgk_ref[...], preferred_element_type=F32) + bgk_ref[...]
    log_a = -(jnp.maximum(-gate, 0.0) + jnp.log1p(jnp.exp(-jnp.abs(gate))))
    log_a = log_a * (1.0 / GLA_GATE_TEMP)
    la_hi = log_a.astype(BF16)
    la_lo = (log_a - la_hi.astype(F32)).astype(BF16)
    tri = tri_ref[...]
    bcum = (jnp.dot(tri, la_hi, preferred_element_type=F32)
            + jnp.dot(tri, la_lo, preferred_element_type=F32))

    lane = lax.broadcasted_iota(jnp.int32, (CHUNK, LANES), 1)
    g_out = g_ref[...]
    for c in range(T // CHUNK):
        r0 = c * CHUNK
        b_c = bcum[r0:r0 + CHUNK, :]
        b_end = b_c[CHUNK - 1:CHUNK, :]
        kdec = k[r0:r0 + CHUNK, :] * jnp.exp(b_end - b_c)
        a_c = jnp.exp(b_end)
        for p in range(GLA_HEADS // 2):
            kd_pair = kdec[:, p * LANES:(p + 1) * LANES]
            q_pair = q[r0:r0 + CHUNK, p * LANES:(p + 1) * LANES]
            a_pair = a_c[:, p * LANES:(p + 1) * LANES]
            for hh in range(2):
                h = 2 * p + hh
                in_head = (lane >= hh * GLA_DK) & (lane < (hh + 1) * GLA_DK)
                kd_m = jnp.where(in_head, kd_pair, 0.0).astype(BF16)
                v_h = v[r0:r0 + CHUNK, h * GLA_DV:(h + 1) * GLA_DV]
                u_t = lax.dot_general(v_h, kd_m, (((0,), (0,)), ((), ())),
                                      preferred_element_type=F32)
                s_new = state[h] * a_pair + u_t
                state[h] = s_new
                o_h = lax.dot_general(q_pair, s_new.astype(BF16), (((1,), (1,)), ((), ())),
                                      preferred_element_type=F32)
                ms = jnp.mean(o_h * o_h, axis=-1, keepdims=True)
                zz = z_ref[0, r0:r0 + CHUNK, h * GLA_DV:(h + 1) * GLA_DV].astype(F32)
                res = o_h * lax.rsqrt(ms + EPS) * g_out * _silu(zz)
                o_ref[0, r0:r0 + CHUNK, h * GLA_DV:(h + 1) * GLA_DV] = res.astype(BF16)


def _gla_branch(gqkv, glr, gz, conv_w8, wgk_pad, bgk, tri, g_out):
    B, S, _ = gqkv.shape
    T = GLA_T
    row = lambda b, t: (b, t, 0)
    const = lambda b, t: (0, 0)
    return pl.pallas_call(
        _gla_kernel,
        out_shape=jax.ShapeDtypeStruct((B, S, GLA_WIDTH), BF16),
        grid=(B, S // T),
        in_specs=[
            pl.BlockSpec((1, T, GLA_CONV_WIDTH), row),
            pl.BlockSpec((1, T, GATE_PAD), row),
            pl.BlockSpec((1, T, GLA_WIDTH), row),
            pl.BlockSpec((8, GLA_CONV_WIDTH), const),
            pl.BlockSpec((GATE_PAD, GLA_QK), const),
            pl.BlockSpec((1, GLA_QK), const),
            pl.BlockSpec((T, T), const),
            pl.BlockSpec((1, GLA_DV), const),
        ],
        out_specs=pl.BlockSpec((1, T, GLA_WIDTH), row),
        scratch_shapes=[
            pltpu.VMEM((T + 8, GLA_CONV_WIDTH), F32),
            pltpu.VMEM((GLA_HEADS, GLA_DV, LANES), F32),
        ],
        compiler_params=pltpu.CompilerParams(
            dimension_semantics=("arbitrary", "arbitrary"), vmem_limit_bytes=VMEM_LIMIT),
        name="gla_branch",
    )(gqkv, glr, gz, conv_w8, wgk_pad, bgk, tri, g_out)


def _split_maps(qt_ref):
    qt = qt_ref[0, 0]
    row = lax.broadcasted_iota(jnp.int32, qt.shape, 0)
    zero = jnp.zeros_like(qt)
    return (jnp.where(row < DIFF_DH, qt, zero), jnp.where(row >= DIFF_DH, qt, zero))


def _chunk_mask():
    kc = lax.broadcasted_iota(jnp.int32, (ATT_T, ATT_T), 0) // CHUNK
    qc = lax.broadcasted_iota(jnp.int32, (ATT_T, ATT_T), 1) // CHUNK
    return kc <= qc


def _values_with_ones(vt_ref, j):
    return jnp.concatenate([vt_ref[0, j], jnp.ones((ATT_LROWS, ATT_T), BF16)], axis=0)


def _attn_finalize(acc_s, z_ref, lq1_ref, lk1_ref, lq2_ref, lk2_ref, li_ref, g_ref, o_ref):
    lam_init = li_ref[:, 0:1]
    lam = (jnp.exp(jnp.sum(lq1_ref[...] * lk1_ref[...], axis=-1, keepdims=True))
           - jnp.exp(jnp.sum(lq2_ref[...] * lk2_ref[...], axis=-1, keepdims=True))
           + lam_init)
    acc0, acc1 = acc_s[0], acc_s[1]
    out = (acc0[0:DIFF_DV] / acc0[DIFF_DV:DIFF_DV + 1]
           - lam * (acc1[0:DIFF_DV] / acc1[DIFF_DV:DIFF_DV + 1]))
    ms = jnp.mean(out * out, axis=0, keepdims=True)
    out = out * lax.rsqrt(ms + EPS) * g_ref[...] * (1.0 - lam_init)
    o_ref[0] = (out.T * _silu(z_ref[0].astype(F32))).astype(BF16)


def _attn_kernel_online(qt_ref, k_ref, vt_ref, z_ref, lq1_ref, lk1_ref, lq2_ref, lk2_ref,
                        li_ref, g_ref, o_ref, s_a, s_b, cm_a, cm_b, m_s, acc_s):
    T = ATT_T
    i = pl.program_id(2)
    qt_maps = _split_maps(qt_ref)

    m_s[...] = jnp.full(m_s.shape, -jnp.inf, F32)
    acc_s[...] = jnp.zeros(acc_s.shape, F32)

    def scores(j, s_ref, cm_ref):
        kj = k_ref[0, pl.ds(pl.multiple_of(j * T, T), T), :]
        for mp in range(2):
            s = jnp.dot(kj, qt_maps[mp], preferred_element_type=F32)
            s_ref[mp] = s
            cm_ref[mp] = jnp.max(s, axis=0, keepdims=True)

    def update(j, s_ref, cm_ref, mask):
        vtj = _values_with_ones(vt_ref, j)
        for mp in range(2):
            s = s_ref[mp]
            if mask is None:
                tile_max = cm_ref[mp]
            else:
                s = jnp.where(mask, s, -jnp.inf)
                tile_max = jnp.max(s, axis=0, keepdims=True)
            m_old = m_s[mp]
            m_new = jnp.maximum(m_old, tile_max)
            alpha = jnp.exp2(m_old - m_new)
            p = jnp.exp2(s - m_new)
            acc_s[mp] = alpha * acc_s[mp] + jnp.dot(vtj, p.astype(BF16),
                                                    preferred_element_type=F32)
            m_s[mp] = m_new

    scores(0, s_a, cm_a)

    def pair(t, carry):
        j = 2 * t
        scores(j + 1, s_b, cm_b)
        update(j, s_a, cm_a, None)
        scores(j + 2, s_a, cm_a)
        update(j + 1, s_b, cm_b, None)
        return carry

    lax.fori_loop(0, i // 2, pair, 0)
    mask = _chunk_mask()

    @pl.when(i % 2 == 0)
    def _():
        update(i, s_a, cm_a, mask)

    @pl.when(i % 2 == 1)
    def _():
        scores(i, s_b, cm_b)
        update(i - 1, s_a, cm_a, None)
        update(i, s_b, cm_b, mask)

    _attn_finalize(acc_s, z_ref, lq1_ref, lk1_ref, lq2_ref, lk2_ref, li_ref, g_ref, o_ref)


def _attn_kernel_bounded(qt_ref, k_ref, vt_ref, z_ref, lq1_ref, lk1_ref, lq2_ref, lk2_ref,
                         li_ref, g_ref, o_ref, acc_s):
    T, U = ATT_T, ATT_UNROLL
    i = pl.program_id(2)
    qt_maps = _split_maps(qt_ref)
    acc_s[...] = jnp.zeros(acc_s.shape, F32)

    def group(j0, n, mask):
        ps = ([], [])
        for u in range(n):
            kj = k_ref[0, pl.ds(pl.multiple_of((j0 + u) * T, T), T), :]
            for mp in range(2):
                p = jnp.exp2(jnp.dot(kj, qt_maps[mp], preferred_element_type=F32))
                if mask is not None and u == n - 1:
                    p = jnp.where(mask, p, 0.0)
                ps[mp].append(p.astype(BF16))
        vt = jnp.concatenate([_values_with_ones(vt_ref, j0 + u) for u in range(n)], axis=1)
        for mp in range(2):
            p_cat = jnp.concatenate(ps[mp], axis=0)
            acc_s[mp] += jnp.dot(vt, p_cat, preferred_element_type=F32)

    def body(t, carry):
        group(t * U, U, None)
        return carry

    lax.fori_loop(0, i // U, body, 0)
    mask = _chunk_mask()
    rem = i % U
    for r in range(U):
        @pl.when(rem == r)
        def _():
            group(i - r, r + 1, mask)

    _attn_finalize(acc_s, z_ref, lq1_ref, lk1_ref, lq2_ref, lk2_ref, li_ref, g_ref, o_ref)


def _diff_attention(bounded, qt, dk, vt, dz, lq1, lk1, lq2, lk2, lam_init_row, g_full):
    B, S, _ = dk.shape
    T = ATT_T
    H = DIFF_HEADS
    nt = S // T
    qmap = lambda b, h, i: (b, i, h)
    const = lambda b, h, i: (0, 0)
    acc = pltpu.VMEM((2, DIFF_DV + ATT_LROWS, T), F32)
    if bounded:
        body, scratch = _attn_kernel_bounded, [acc]
    else:
        body = _attn_kernel_online
        scratch = [pltpu.VMEM((2, T, T), F32), pltpu.VMEM((2, T, T), F32),
                   pltpu.VMEM((2, 1, T), F32), pltpu.VMEM((2, 1, T), F32),
                   pltpu.VMEM((2, 1, T), F32), acc]
    return pl.pallas_call(
        body,
        out_shape=jax.ShapeDtypeStruct((B, S, DIFF_WIDTH), BF16),
        grid=(B, H, nt),
        in_specs=[
            pl.BlockSpec((1, 1, 2 * DIFF_DH, T), lambda b, h, i: (b, i, h, 0)),
            pl.BlockSpec((1, S, LANES), lambda b, h, i: (b, 0, h)),
            pl.BlockSpec((1, nt, DIFF_DV, T), lambda b, h, i: (b, 0, h, 0)),
            pl.BlockSpec((1, T, LANES), qmap),
            pl.BlockSpec((1, DIFF_DH), const),
            pl.BlockSpec((1, DIFF_DH), const),
            pl.BlockSpec((1, DIFF_DH), const),
            pl.BlockSpec((1, DIFF_DH), const),
            pl.BlockSpec((1, LANES), const),
            pl.BlockSpec((DIFF_DV, T), const),
        ],
        out_specs=pl.BlockSpec((1, T, LANES), qmap),
        scratch_shapes=scratch,
        compiler_params=pltpu.CompilerParams(
            dimension_semantics=("arbitrary", "arbitrary", "arbitrary"),
            vmem_limit_bytes=VMEM_LIMIT),
        name="diff_attn_bounded" if bounded else "diff_attn_online",
    )(qt, dk, vt, dz, lq1, lk1, lq2, lk2, lam_init_row, g_full)


def _out_kernel(og_ref, od_ref, x_ref, mod_ref, w_ref, o_ref):
    D = D_MODEL
    gate = mod_ref[0][:, 2 * D:3 * D]
    y = (jnp.dot(og_ref[0], w_ref[0:GLA_WIDTH, :], preferred_element_type=F32)
         + jnp.dot(od_ref[0], w_ref[GLA_WIDTH:MIX_WIDTH, :], preferred_element_type=F32))
    o_ref[0] = x_ref[0] + gate * y


def _out_projection(o_gla, o_diff, x, mod_l, w_out):
    B, S, D = x.shape
    tm = OUT_TM
    row = lambda b, i: (b, i, 0)
    return pl.pallas_call(
        _out_kernel,
        out_shape=jax.ShapeDtypeStruct((B, S, D), F32),
        grid=(B, S // tm),
        in_specs=[
            pl.BlockSpec((1, tm, GLA_WIDTH), row),
            pl.BlockSpec((1, tm, DIFF_WIDTH), row),
            pl.BlockSpec((1, tm, D), row),
            pl.BlockSpec((1, 1, 3 * D), lambda b, i: (b, 0, 0)),
            pl.BlockSpec((MIX_WIDTH, D), lambda b, i: (0, 0), pipeline_mode=pl.Buffered(1)),
        ],
        out_specs=pl.BlockSpec((1, tm, D), row),
        compiler_params=pltpu.CompilerParams(
            dimension_semantics=("arbitrary", "arbitrary"), vmem_limit_bytes=VMEM_LIMIT),
        name="out_proj",
    )(o_gla, o_diff, x, mod_l, w_out)


def _split_w_in(w):
    idx = [0]
    for s in SPLIT_SIZES:
        idx.append(idx[-1] + s)
    gq, gk, gv, glr, gz, dq, dk, dv, dz = [w[:, idx[n]:idx[n + 1]] for n in range(9)]
    glr = jnp.pad(glr, ((0, 0), (0, GATE_PAD - GLA_GATE_RANK)))
    w_nat = jnp.concatenate([gq, gk, gv, gz, dk, dz, glr], axis=1).astype(BF16)
    w_t = jnp.concatenate([dq, dv], axis=1).T.astype(BF16)
    return w_nat, w_t


def kernel(x, c, w_ada, b_ada, norm_g, w_in, conv_w, w_gk, b_gk, gla_norm_g,
           qn_g, kn_g, lam_q1, lam_k1, lam_q2, lam_k2, diff_norm_g, w_out):
    B, S, D = x.shape
    mod = _modulation(c, w_ada, b_ada)

    r = jnp.arange(DIFF_QK) // DIFF_DH
    ones_bd = (r[:, None] == r[None, :]).astype(BF16)
    t = jnp.arange(GLA_T)
    tri = ((t[:, None] // CHUNK == t[None, :] // CHUNK)
           & (t[None, :] <= t[:, None])).astype(BF16)

    for l in range(DEPTH):
        mod_l = mod[l].reshape(B, 1, 3 * D)
        w_nat, w_t = _split_w_in(w_in[l])
        qg_full = jnp.broadcast_to(
            jnp.tile(qn_g[l], DIFF_QK // DIFF_DH)[:, None], (DIFF_QK, PROJ_TM))
        kg = jnp.tile(kn_g[l], DIFF_QK // DIFF_DH).reshape(1, DIFF_QK)
        gqkv, gz, dk, dz, glr, qt, vt = _in_projection(
            x, mod_l, norm_g[l].reshape(1, D), w_nat, w_t, ones_bd, qg_full, kg)

        conv_w8 = jnp.pad(conv_w[l], ((0, 8 - CONV_K), (0, 0)))
        wgk_pad = jnp.pad(w_gk[l], ((0, GATE_PAD - GLA_GATE_RANK), (0, 0))).astype(BF16)
        o_gla = _gla_branch(gqkv, glr, gz, conv_w8, wgk_pad, b_gk[l].reshape(1, GLA_QK),
                            tri, gla_norm_g[l].reshape(1, GLA_DV))

        lam_init = 0.8 - 0.6 * math.exp(-0.3 * l)
        score_bound = (DIFF_DH ** 0.5 * LOG2E * 1.01
                       * jnp.max(jnp.abs(qn_g[l])) * jnp.max(jnp.abs(kn_g[l])))
        attn_args = (
            qt, dk, vt, dz,
            lam_q1[l].reshape(1, DIFF_DH), lam_k1[l].reshape(1, DIFF_DH),
            lam_q2[l].reshape(1, DIFF_DH), lam_k2[l].reshape(1, DIFF_DH),
            jnp.full((1, LANES), lam_init, F32),
            jnp.broadcast_to(diff_norm_g[l][:, None], (DIFF_DV, ATT_T)))
        o_diff = lax.cond(score_bound <= ATT_SCORE_BOUND,
                          functools.partial(_diff_attention, True),
                          functools.partial(_diff_attention, False), *attn_args)

        x = _out_projection(o_gla, o_diff, x, mod_l, w_out[l].astype(BF16))
    return x
```

```python
import functools
import math

import jax
import jax.numpy as jnp
from jax import lax
from jax.experimental import pallas as pl
from jax.experimental.pallas import tpu as pltpu

D_MODEL = 1024
DEPTH = 4
CHUNK = 64
CONV_K = 4
EPS = 1e-6

GLA_HEADS = 4
GLA_DK = 64
GLA_DV = 128
GLA_QK = GLA_HEADS * GLA_DK
GLA_WIDTH = GLA_HEADS * GLA_DV
GLA_GATE_RANK = 16
GLA_GATE_TEMP = 16.0

DIFF_HEADS = 4
DIFF_DH = 64
DIFF_DV = 2 * DIFF_DH
DIFF_QK = DIFF_HEADS * 2 * DIFF_DH
DIFF_WIDTH = DIFF_HEADS * DIFF_DV

MIX_WIDTH = GLA_WIDTH + DIFF_WIDTH
GLA_CONV_WIDTH = 2 * GLA_QK + GLA_WIDTH
SPLIT_SIZES = (GLA_QK, GLA_QK, GLA_WIDTH, GLA_GATE_RANK, GLA_WIDTH,
               DIFF_QK, DIFF_QK, DIFF_WIDTH, DIFF_WIDTH)

LANES = 128
GATE_PAD = LANES
VMEM_LIMIT = 56 * 1024 * 1024

ATT_T = 512
ATT_UNROLL = 4
ATT_SCORE_BOUND = 32.0
ATT_LROWS = 16
PROJ_TM = ATT_T
OUT_TM = 512
GLA_T = 512
MOD_TN = 512

F32 = jnp.float32
BF16 = jnp.bfloat16
LOG2E = math.log2(math.e)


def _sigmoid(x):
    return 1.0 / (1.0 + jnp.exp(-x))


def _silu(x):
    return x * _sigmoid(x)


def _mod_kernel(ct_ref, w_ref, b_ref, o_ref):
    ct = ct_ref[...]
    ca = _silu(ct)
    w = w_ref[0]
    rows = []
    for b in range(ct.shape[1]):
        rows.append(jnp.sum(w * ca[:, b:b + 1], axis=0, keepdims=True))
    o_ref[0] = jnp.concatenate(rows, axis=0) + b_ref[0]


def _modulation(c, w_ada, b_ada):
    B, D = c.shape
    L, _, N = w_ada.shape
    return pl.pallas_call(
        _mod_kernel,
        out_shape=jax.ShapeDtypeStruct((L, B, N), F32),
        grid=(L, N // MOD_TN),
        in_specs=[
            pl.BlockSpec((D, B), lambda l, n: (0, 0)),
            pl.BlockSpec((1, D, MOD_TN), lambda l, n: (l, 0, n)),
            pl.BlockSpec((1, 1, MOD_TN), lambda l, n: (l, 0, n)),
        ],
        out_specs=pl.BlockSpec((1, B, MOD_TN), lambda l, n: (l, 0, n)),
        compiler_params=pltpu.CompilerParams(
            dimension_semantics=("arbitrary", "arbitrary"), vmem_limit_bytes=VMEM_LIMIT),
        name="adaln_mod",
    )(c.T, w_ada, b_ada.reshape(L, 1, N))


def _group_rms(t, ones_bd, gain):
    ss = jnp.dot((t * t).astype(BF16), ones_bd, preferred_element_type=F32)
    return t * lax.rsqrt(ss * (1.0 / DIFF_DH) + EPS) * gain


def _proj_kernel(x_ref, mod_ref, g_ref, w_ref, wt_ref, ones_ref, qg_ref, kg_ref,
                 gqkv_ref, gz_ref, dk_ref, dz_ref, glr_ref, qt_ref, vt_ref):
    D = D_MODEL
    x = x_ref[0]
    mod = mod_ref[0]
    shift = mod[:, 0:D]
    scale = mod[:, D:2 * D]
    ms = jnp.mean(x * x, axis=-1, keepdims=True)
    y = x * lax.rsqrt(ms + EPS) * g_ref[...]
    h = (y * (1.0 + scale) + shift).astype(BF16)

    def proj(a, n):
        return jnp.dot(h, w_ref[:, a:a + n], preferred_element_type=F32)

    def proj_t(a, n):
        return lax.dot_general(wt_ref[a:a + n, :], h, (((1,), (1,)), ((), ())),
                               preferred_element_type=F32)

    gqkv_ref[0, :, 0:512] = proj(0, 512).astype(BF16)
    gqkv_ref[0, :, 512:1024] = proj(512, 512).astype(BF16)
    gz_ref[0] = proj(1024, 512).astype(BF16)
    dk_ref[0] = _group_rms(proj(1536, 512), ones_ref[...], kg_ref[...]).astype(BF16)
    dz_ref[0] = proj(2048, 512).astype(BF16)
    glr_ref[0] = proj(2560, GATE_PAD).astype(BF16)

    qt = proj_t(0, DIFF_QK)
    for g in range(DIFF_QK // DIFF_DH):
        r0 = g * DIFF_DH
        t = qt[r0:r0 + DIFF_DH, :]
        ss = jnp.sum(t * t, axis=0, keepdims=True)
        t = t * lax.rsqrt(ss * (1.0 / DIFF_DH) + EPS) * qg_ref[r0:r0 + DIFF_DH, :]
        qt_ref[0, 0, r0:r0 + DIFF_DH, :] = (t * (DIFF_DH ** -0.5 * LOG2E)).astype(BF16)
    vt_ref[0, 0] = proj_t(DIFF_QK, DIFF_WIDTH).astype(BF16)


def _in_projection(x, mod_l, norm_g, w_nat, w_t, ones_bd, qg_full, kg):
    B, S, D = x.shape
    tm = PROJ_TM
    NW = w_nat.shape[1]
    row = lambda b, i: (b, i, 0)
    tile = lambda b, i: (b, i, 0, 0)
    const = lambda b, i: (0, 0)
    widths = (1024, 512, 512, 512, GATE_PAD)
    once = pl.Buffered(1)
    return pl.pallas_call(
        _proj_kernel,
        out_shape=([jax.ShapeDtypeStruct((B, S, w), BF16) for w in widths]
                   + [jax.ShapeDtypeStruct((B, S // tm, DIFF_QK, tm), BF16),
                      jax.ShapeDtypeStruct((B, S // tm, DIFF_WIDTH, tm), BF16)]),
        grid=(B, S // tm),
        in_specs=[
            pl.BlockSpec((1, tm, D), row),
            pl.BlockSpec((1, 1, 3 * D), lambda b, i: (b, 0, 0)),
            pl.BlockSpec((1, D), const),
            pl.BlockSpec((D, NW), const, pipeline_mode=once),
            pl.BlockSpec((DIFF_QK + DIFF_WIDTH, D), const, pipeline_mode=once),
            pl.BlockSpec((512, 512), const, pipeline_mode=once),
            pl.BlockSpec((DIFF_QK, tm), const, pipeline_mode=once),
            pl.BlockSpec((1, 512), const),
        ],
        out_specs=([pl.BlockSpec((1, tm, w), row) for w in widths]
                   + [pl.BlockSpec((1, 1, DIFF_QK, tm), tile),
                      pl.BlockSpec((1, 1, DIFF_WIDTH, tm), tile)]),
        compiler_params=pltpu.CompilerParams(
            dimension_semantics=("arbitrary", "arbitrary"), vmem_limit_bytes=VMEM_LIMIT),
        name="in_proj",
    )(x, mod_l, norm_g, w_nat, w_t, ones_bd, qg_full, kg)


def _gla_kernel(u_ref, glr_ref, z_ref, cw_ref, wgk_ref, bgk_ref, tri_ref, g_ref,
                o_ref, xbuf, state):
    T = GLA_T
    t_idx = pl.program_id(1)

    @pl.when(t_idx == 0)
    def _():
        xbuf[0:8, :] = jnp.zeros((8, GLA_CONV_WIDTH), F32)
        state[...] = jnp.zeros(state.shape, F32)

    xbuf[8:8 + T, :] = u_ref[0].astype(F32)
    cw = cw_ref[...]
    conv = cw[0:1, :] * xbuf[pl.ds(8 - (CONV_K - 1), T), :]
    for j in range(1, CONV_K):
        conv = conv + cw[j:j + 1, :] * xbuf[pl.ds(8 - (CONV_K - 1) + j, T), :]
    xbuf[0:8, :] = xbuf[T:T + 8, :]
    act = _silu(conv)
    q = (act[:, 0:GLA_QK] * (GLA_DK ** -0.5)).astype(BF16)
    k = act[:, GLA_QK:2 * GLA_QK]
    v = act[:, 2 * GLA_QK:].astype(BF16)

    gate = jnp.dot(glr_ref[0], wgk_ref[...], preferred_element_type=F32) + bgk_ref[...]
    log_a = -(jnp.maximum(-gate, 0.0) + jnp.log1p(jnp.exp(-jnp.abs(gate))))
    log_a = log_a * (1.0 / GLA_GATE_TEMP)
    la_hi = log_a.astype(BF16)
    la_lo = (log_a - la_hi.astype(F32)).astype(BF16)
    tri = tri_ref[...]
    bcum = (jnp.dot(tri, la_hi, preferred_element_type=F32)
            + jnp.dot(tri, la_lo, preferred_element_type=F32))

    lane = lax.broadcasted_iota(jnp.int32, (CHUNK, LANES), 1)
    g_out = g_ref[...]
    for c in range(T // CHUNK):
        r0 = c * CHUNK
        b_c = bcum[r0:r0 + CHUNK, :]
        b_end = b_c[CHUNK - 1:CHUNK, :]
        kdec = k[r0:r0 + CHUNK, :] * jnp.exp(b_end - b_c)
        a_c = jnp.exp(b_end)
        for p in range(GLA_HEADS // 2):
            kd_pair = kdec[:, p * LANES:(p + 1) * LANES]
            q_pair = q[r0:r0 + CHUNK, p * LANES:(p + 1) * LANES]
            a_pair = a_c[:, p * LANES:(p + 1) * LANES]
            for hh in range(2):
                h = 2 * p + hh
                in_head = (lane >= hh * GLA_DK) & (lane < (hh + 1) * GLA_DK)
                kd_m = jnp.where(in_head, kd_pair, 0.0).astype(BF16)
                v_h = v[r0:r0 + CHUNK, h * GLA_DV:(h + 1) * GLA_DV]
                u_t = lax.dot_general(v_h, kd_m, (((0,), (0,)), ((), ())),
                                      preferred_element_type=F32)
                s_new = state[h] * a_pair + u_t
                state[h] = s_new
                o_h = lax.dot_general(q_pair, s_new.astype(BF16), (((1,), (1,)), ((), ())),
                                      preferred_element_type=F32)
                ms = jnp.mean(o_h * o_h, axis=-1, keepdims=True)
                zz = z_ref[0, r0:r0 + CHUNK, h * GLA_DV:(h + 1) * GLA_DV].astype(F32)
                res = o_h * lax.rsqrt(ms + EPS) * g_out * _silu(zz)
                o_ref[0, r0:r0 + CHUNK, h * GLA_DV:(h + 1) * GLA_DV] = res.astype(BF16)


def _gla_branch(gqkv, glr, gz, conv_w8, wgk_pad, bgk, tri, g_out):
    B, S, _ = gqkv.shape
    T = GLA_T
    row = lambda b, t: (b, t, 0)
    const = lambda b, t: (0, 0)
    return pl.pallas_call(
        _gla_kernel,
        out_shape=jax.ShapeDtypeStruct((B, S, GLA_WIDTH), BF16),
        grid=(B, S // T),
        in_specs=[
            pl.BlockSpec((1, T, GLA_CONV_WIDTH), row),
            pl.BlockSpec((1, T, GATE_PAD), row),
            pl.BlockSpec((1, T, GLA_WIDTH), row),
            pl.BlockSpec((8, GLA_CONV_WIDTH), const),
            pl.BlockSpec((GATE_PAD, GLA_QK), const),
            pl.BlockSpec((1, GLA_QK), const),
            pl.BlockSpec((T, T), const),
            pl.BlockSpec((1, GLA_DV), const),
        ],
        out_specs=pl.BlockSpec((1, T, GLA_WIDTH), row),
        scratch_shapes=[
            pltpu.VMEM((T + 8, GLA_CONV_WIDTH), F32),
            pltpu.VMEM((GLA_HEADS, GLA_DV, LANES), F32),
        ],
        compiler_params=pltpu.CompilerParams(
            dimension_semantics=("arbitrary", "arbitrary"), vmem_limit_bytes=VMEM_LIMIT),
        name="gla_branch",
    )(gqkv, glr, gz, conv_w8, wgk_pad, bgk, tri, g_out)


def _split_maps(qt_ref):
    qt = qt_ref[0, 0]
    row = lax.broadcasted_iota(jnp.int32, qt.shape, 0)
    zero = jnp.zeros_like(qt)
    return (jnp.where(row < DIFF_DH, qt, zero), jnp.where(row >= DIFF_DH, qt, zero))


def _chunk_mask():
    kc = lax.broadcasted_iota(jnp.int32, (ATT_T, ATT_T), 0) // CHUNK
    qc = lax.broadcasted_iota(jnp.int32, (ATT_T, ATT_T), 1) // CHUNK
    return kc <= qc


def _values_with_ones(vt_ref, j):
    return jnp.concatenate([vt_ref[0, j], jnp.ones((ATT_LROWS, ATT_T), BF16)], axis=0)


def _attn_finalize(acc_s, z_ref, lq1_ref, lk1_ref, lq2_ref, lk2_ref, li_ref, g_ref, o_ref):
    lam_init = li_ref[:, 0:1]
    lam = (jnp.exp(jnp.sum(lq1_ref[...] * lk1_ref[...], axis=-1, keepdims=True))
           - jnp.exp(jnp.sum(lq2_ref[...] * lk2_ref[...], axis=-1, keepdims=True))
           + lam_init)
    acc0, acc1 = acc_s[0], acc_s[1]
    out = (acc0[0:DIFF_DV] / acc0[DIFF_DV:DIFF_DV + 1]
           - lam * (acc1[0:DIFF_DV] / acc1[DIFF_DV:DIFF_DV + 1]))
    ms = jnp.mean(out * out, axis=0, keepdims=True)
    out = out * lax.rsqrt(ms + EPS) * g_ref[...] * (1.0 - lam_init)
    o_ref[0] = (out.T * _silu(z_ref[0].astype(F32))).astype(BF16)


def _attn_kernel_online(qt_ref, k_ref, vt_ref, z_ref, lq1_ref, lk1_ref, lq2_ref, lk2_ref,
                        li_ref, g_ref, o_ref, s_a, s_b, cm_a, cm_b, m_s, acc_s):
    T = ATT_T
    i = pl.program_id(2)
    qt_maps = _split_maps(qt_ref)

    m_s[...] = jnp.full(m_s.shape, -jnp.inf, F32)
    acc_s[...] = jnp.zeros(acc_s.shape, F32)

    def scores(j, s_ref, cm_ref):
        kj = k_ref[0, pl.ds(pl.multiple_of(j * T, T), T), :]
        for mp in range(2):
            s = jnp.dot(kj, qt_maps[mp], preferred_element_type=F32)
            s_ref[mp] = s
            cm_ref[mp] = jnp.max(s, axis=0, keepdims=True)

    def update(j, s_ref, cm_ref, mask):
        vtj = _values_with_ones(vt_ref, j)
        for mp in range(2):
            s = s_ref[mp]
            if mask is None:
                tile_max = cm_ref[mp]
            else:
                s = jnp.where(mask, s, -jnp.inf)
                tile_max = jnp.max(s, axis=0, keepdims=True)
            m_old = m_s[mp]
            m_new = jnp.maximum(m_old, tile_max)
            alpha = jnp.exp2(m_old - m_new)
            p = jnp.exp2(s - m_new)
            acc_s[mp] = alpha * acc_s[mp] + jnp.dot(vtj, p.astype(BF16),
                                                    preferred_element_type=F32)
            m_s[mp] = m_new

    scores(0, s_a, cm_a)

    def pair(t, carry):
        j = 2 * t
        scores(j + 1, s_b, cm_b)
        update(j, s_a, cm_a, None)
        scores(j + 2, s_a, cm_a)
        update(j + 1, s_b, cm_b, None)
        return carry

    lax.fori_loop(0, i // 2, pair, 0)
    mask = _chunk_mask()

    @pl.when(i % 2 == 0)
    def _():
        update(i, s_a, cm_a, mask)

    @pl.when(i % 2 == 1)
    def _():
        scores(i, s_b, cm_b)
        update(i - 1, s_a, cm_a, None)
        update(i, s_b, cm_b, mask)

    _attn_finalize(acc_s, z_ref, lq1_ref, lk1_ref, lq2_ref, lk2_ref, li_ref, g_ref, o_ref)


def _attn_kernel_bounded(qt_ref, k_ref, vt_ref, z_ref, lq1_ref, lk1_ref, lq2_ref, lk2_ref,
                         li_ref, g_ref, o_ref, acc_s):
    T, U = ATT_T, ATT_UNROLL
    i = pl.program_id(2)
    qt_maps = _split_maps(qt_ref)
    acc_s[...] = jnp.zeros(acc_s.shape, F32)

    def group(j0, n, mask):
        ps = ([], [])
        for u in range(n):
            kj = k_ref[0, pl.ds(pl.multiple_of((j0 + u) * T, T), T), :]
            for mp in range(2):
                p = jnp.exp2(jnp.dot(kj, qt_maps[mp], preferred_element_type=F32))
                if mask is not None and u == n - 1:
                    p = jnp.where(mask, p, 0.0)
                ps[mp].append(p.astype(BF16))
        vt = jnp.concatenate([_values_with_ones(vt_ref, j0 + u) for u in range(n)], axis=1)
        for mp in range(2):
            p_cat = jnp.concatenate(ps[mp], axis=0)
            acc_s[mp] += jnp.dot(vt, p_cat, preferred_element_type=F32)

    def body(t, carry):
        group(t * U, U, None)
        return carry

    lax.fori_loop(0, i // U, body, 0)
    mask = _chunk_mask()
    rem = i % U
    for r in range(U):
        @pl.when(rem == r)
        def _():
            group(i - r, r + 1, mask)

    _attn_finalize(acc_s, z_ref, lq1_ref, lk1_ref, lq2_ref, lk2_ref, li_ref, g_ref, o_ref)


def _diff_attention(bounded, qt, dk, vt, dz, lq1, lk1, lq2, lk2, lam_init_row, g_full):
    B, S, _ = dk.shape
    T = ATT_T
    H = DIFF_HEADS
    nt = S // T
    qmap = lambda b, h, i: (b, i, h)
    const = lambda b, h, i: (0, 0)
    acc = pltpu.VMEM((2, DIFF_DV + ATT_LROWS, T), F32)
    if bounded:
        body, scratch = _attn_kernel_bounded, [acc]
    else:
        body = _attn_kernel_online
        scratch = [pltpu.VMEM((2, T, T), F32), pltpu.VMEM((2, T, T), F32),
                   pltpu.VMEM((2, 1, T), F32), pltpu.VMEM((2, 1, T), F32),
                   pltpu.VMEM((2, 1, T), F32), acc]
    return pl.pallas_call(
        body,
        out_shape=jax.ShapeDtypeStruct((B, S, DIFF_WIDTH), BF16),
        grid=(B, H, nt),
        in_specs=[
            pl.BlockSpec((1, 1, 2 * DIFF_DH, T), lambda b, h, i: (b, i, h, 0)),
            pl.BlockSpec((1, S, LANES), lambda b, h, i: (b, 0, h)),
            pl.BlockSpec((1, nt, DIFF_DV, T), lambda b, h, i: (b, 0, h, 0)),
            pl.BlockSpec((1, T, LANES), qmap),
            pl.BlockSpec((1, DIFF_DH), const),
            pl.BlockSpec((1, DIFF_DH), const),
            pl.BlockSpec((1, DIFF_DH), const),
            pl.BlockSpec((1, DIFF_DH), const),
            pl.BlockSpec((1, LANES), const),
            pl.BlockSpec((DIFF_DV, T), const),
        ],
        out_specs=pl.BlockSpec((1, T, LANES), qmap),
        scratch_shapes=scratch,
        compiler_params=pltpu.CompilerParams(
            dimension_semantics=("arbitrary", "arbitrary", "arbitrary"),
            vmem_limit_bytes=VMEM_LIMIT),
        name="diff_attn_bounded" if bounded else "diff_attn_online",
    )(qt, dk, vt, dz, lq1, lk1, lq2, lk2, lam_init_row, g_full)


def _out_kernel(og_ref, od_ref, x_ref, mod_ref, w_ref, o_ref):
    D = D_MODEL
    gate = mod_ref[0][:, 2 * D:3 * D]
    y = (jnp.dot(og_ref[0], w_ref[0:GLA_WIDTH, :], preferred_element_type=F32)
         + jnp.dot(od_ref[0], w_ref[GLA_WIDTH:MIX_WIDTH, :], preferred_element_type=F32))
    o_ref[0] = x_ref[0] + gate * y


def _out_projection(o_gla, o_diff, x, mod_l, w_out):
    B, S, D = x.shape
    tm = OUT_TM
    row = lambda b, i: (b, i, 0)
    return pl.pallas_call(
        _out_kernel,
        out_shape=jax.ShapeDtypeStruct((B, S, D), F32),
        grid=(B, S // tm),
        in_specs=[
            pl.BlockSpec((1, tm, GLA_WIDTH), row),
            pl.BlockSpec((1, tm, DIFF_WIDTH), row),
            pl.BlockSpec((1, tm, D), row),
            pl.BlockSpec((1, 1, 3 * D), lambda b, i: (b, 0, 0)),
            pl.BlockSpec((MIX_WIDTH, D), lambda b, i: (0, 0), pipeline_mode=pl.Buffered(1)),
        ],
        out_specs=pl.BlockSpec((1, tm, D), row),
        compiler_params=pltpu.CompilerParams(
            dimension_semantics=("arbitrary", "arbitrary"), vmem_limit_bytes=VMEM_LIMIT),
        name="out_proj",
    )(o_gla, o_diff, x, mod_l, w_out)


def _split_w_in(w):
    idx = [0]
    for s in SPLIT_SIZES:
        idx.append(idx[-1] + s)
    gq, gk, gv, glr, gz, dq, dk, dv, dz = [w[:, idx[n]:idx[n + 1]] for n in range(9)]
    glr = jnp.pad(glr, ((0, 0), (0, GATE_PAD - GLA_GATE_RANK)))
    w_nat = jnp.concatenate([gq, gk, gv, gz, dk, dz, glr], axis=1).astype(BF16)
    w_t = jnp.concatenate([dq, dv], axis=1).T.astype(BF16)
    return w_nat, w_t


def kernel(x, c, w_ada, b_ada, norm_g, w_in, conv_w, w_gk, b_gk, gla_norm_g,
           qn_g, kn_g, lam_q1, lam_k1, lam_q2, lam_k2, diff_norm_g, w_out):
    B, S, D = x.shape
    mod = _modulation(c, w_ada, b_ada)

    r = jnp.arange(DIFF_QK) // DIFF_DH
    ones_bd = (r[:, None] == r[None, :]).astype(BF16)
    t = jnp.arange(GLA_T)
    tri = ((t[:, None] // CHUNK == t[None, :] // CHUNK)
           & (t[None, :] <= t[:, None])).astype(BF16)

    for l in range(DEPTH):
        mod_l = mod[l].reshape(B, 1, 3 * D)
        w_nat, w_t = _split_w_in(w_in[l])
        qg_full = jnp.broadcast_to(
            jnp.tile(qn_g[l], DIFF_QK // DIFF_DH)[:, None], (DIFF_QK, PROJ_TM))
        kg = jnp.tile(kn_g[l], DIFF_QK // DIFF_DH).reshape(1, DIFF_QK)
        gqkv, gz, dk, dz, glr, qt, vt = _in_projection(
            x, mod_l, norm_g[l].reshape(1, D), w_nat, w_t, ones_bd, qg_full, kg)

        conv_w8 = jnp.pad(conv_w[l], ((0, 8 - CONV_K), (0, 0)))
        wgk_pad = jnp.pad(w_gk[l], ((0, GATE_PAD - GLA_GATE_RANK), (0, 0))).astype(BF16)
        o_gla = _gla_branch(gqkv, glr, gz, conv_w8, wgk_pad, b_gk[l].reshape(1, GLA_QK),
                            tri, gla_norm_g[l].reshape(1, GLA_DV))

        lam_init = 0.8 - 0.6 * math.exp(-0.3 * l)
        score_bound = (DIFF_DH ** 0.5 * LOG2E * 1.01
                       * jnp.max(jnp.abs(qn_g[l])) * jnp.max(jnp.abs(kn_g[l])))
        attn_args = (
            qt, dk, vt, dz,
            lam_q1[l].reshape(1, DIFF_DH), lam_k1[l].reshape(1, DIFF_DH),
            lam_q2[l].reshape(1, DIFF_DH), lam_k2[l].reshape(1, DIFF_DH),
            jnp.full((1, LANES), lam_init, F32),
            jnp.broadcast_to(diff_norm_g[l][:, None], (DIFF_DV, ATT_T)))
        o_diff = lax.cond(score_bound <= ATT_SCORE_BOUND,
                          functools.partial(_diff_attention, True),
                          functools.partial(_diff_attention, False), *attn_args)

        x = _out_projection(o_gla, o_diff, x, mod_l, w_out[l].astype(BF16))
    return x
```

```python
import functools
import math

import jax
import jax.numpy as jnp
from jax import lax
from jax.experimental import pallas as pl
from jax.experimental.pallas import tpu as pltpu

D_MODEL = 1024
DEPTH = 4
CHUNK = 64
CONV_K = 4
EPS = 1e-6

GLA_HEADS = 4
GLA_DK = 64
GLA_DV = 128
GLA_QK = GLA_HEADS * GLA_DK
GLA_WIDTH = GLA_HEADS * GLA_DV
GLA_GATE_RANK = 16
GLA_GATE_TEMP = 16.0

DIFF_HEADS = 4
DIFF_DH = 64
DIFF_DV = 2 * DIFF_DH
DIFF_QK = DIFF_HEADS * 2 * DIFF_DH
DIFF_WIDTH = DIFF_HEADS * DIFF_DV

MIX_WIDTH = GLA_WIDTH + DIFF_WIDTH
GLA_CONV_WIDTH = 2 * GLA_QK + GLA_WIDTH
SPLIT_SIZES = (GLA_QK, GLA_QK, GLA_WIDTH, GLA_GATE_RANK, GLA_WIDTH,
               DIFF_QK, DIFF_QK, DIFF_WIDTH, DIFF_WIDTH)

LANES = 128
GATE_PAD = LANES
VMEM_LIMIT = 56 * 1024 * 1024

ATT_T = 512
ATT_UNROLL = 4
ATT_SCORE_BOUND = 32.0
ATT_LROWS = 16
PROJ_TM = ATT_T
OUT_TM = 512
MOD_TN = 512

F32 = jnp.float32
BF16 = jnp.bfloat16
LOG2E = math.log2(math.e)


def _sigmoid(x):
    return 1.0 / (1.0 + jnp.exp(-x))


def _silu(x):
    return x * _sigmoid(x)


def _mod_kernel(ct_ref, w_ref, b_ref, o_ref):
    ct = ct_ref[...]
    ca = _silu(ct)
    w = w_ref[0]
    rows = []
    for b in range(ct.shape[1]):
        rows.append(jnp.sum(w * ca[:, b:b + 1], axis=0, keepdims=True))
    o_ref[0] = jnp.concatenate(rows, axis=0) + b_ref[0]


def _modulation(c, w_ada, b_ada):
    B, D = c.shape
    L, _, N = w_ada.shape
    return pl.pallas_call(
        _mod_kernel,
        out_shape=jax.ShapeDtypeStruct((L, B, N), F32),
        grid=(L, N // MOD_TN),
        in_specs=[
            pl.BlockSpec((D, B), lambda l, n: (0, 0)),
            pl.BlockSpec((1, D, MOD_TN), lambda l, n: (l, 0, n)),
            pl.BlockSpec((1, 1, MOD_TN), lambda l, n: (l, 0, n)),
        ],
        out_specs=pl.BlockSpec((1, B, MOD_TN), lambda l, n: (l, 0, n)),
        compiler_params=pltpu.CompilerParams(
            dimension_semantics=("arbitrary", "arbitrary"), vmem_limit_bytes=VMEM_LIMIT),
        name="adaln_mod",
    )(c.T, w_ada, b_ada.reshape(L, 1, N))


def _group_rms(t, ones_bd, gain):
    ss = jnp.dot((t * t).astype(BF16), ones_bd, preferred_element_type=F32)
    return t * lax.rsqrt(ss * (1.0 / DIFF_DH) + EPS) * gain


def _gla_log_decay(gate, tri_ref):
    log_a = -(jnp.maximum(-gate, 0.0) + jnp.log1p(jnp.exp(-jnp.abs(gate))))
    log_a = log_a * (1.0 / GLA_GATE_TEMP)
    la_hi = log_a.astype(BF16)
    la_lo = (log_a - la_hi.astype(F32)).astype(BF16)
    tri = tri_ref[...]
    return (jnp.dot(tri, la_hi, preferred_element_type=F32)
            + jnp.dot(tri, la_lo, preferred_element_type=F32))


def _gla_tile(xbuf, zbuf, state, bcum, cw_ref, g_ref, o_ref):
    T = PROJ_TM
    n_chunks = T // CHUNK
    cw = cw_ref[...]
    conv = cw[0:1, :] * xbuf[pl.ds(8 - (CONV_K - 1), T), :]
    for j in range(1, CONV_K):
        conv = conv + cw[j:j + 1, :] * xbuf[pl.ds(8 - (CONV_K - 1) + j, T), :]
    xbuf[0:8, :] = xbuf[T:T + 8, :]
    act = _silu(conv)
    q = (act[:, 0:GLA_QK] * (GLA_DK ** -0.5)).astype(BF16)
    k = act[:, GLA_QK:2 * GLA_QK]
    v = act[:, 2 * GLA_QK:].astype(BF16)

    lane = lax.broadcasted_iota(jnp.int32, (CHUNK, LANES), 1)
    head_lanes = [(lane >= hh * GLA_DK) & (lane < (hh + 1) * GLA_DK) for hh in range(2)]

    decay, outer = [], []
    for c in range(n_chunks):
        r0 = c * CHUNK
        b_c = bcum[r0:r0 + CHUNK, :]
        b_end = b_c[CHUNK - 1:CHUNK, :]
        kdec = k[r0:r0 + CHUNK, :] * jnp.exp(b_end - b_c)
        decay.append(jnp.exp(b_end))
        for h in range(GLA_HEADS):
            p = h // 2
            kd_m = jnp.where(head_lanes[h % 2], kdec[:, p * LANES:(p + 1) * LANES], 0.0)
            v_h = v[r0:r0 + CHUNK, h * GLA_DV:(h + 1) * GLA_DV]
            outer.append(lax.dot_general(v_h, kd_m.astype(BF16), (((0,), (0,)), ((), ())),
                                         preferred_element_type=F32))

    g_out = g_ref[...]
    for c in range(n_chunks):
        r0 = c * CHUNK
        for h in range(GLA_HEADS):
            p = h // 2
            s_new = state[h] * decay[c][:, p * LANES:(p + 1) * LANES] + outer[c * GLA_HEADS + h]
            state[h] = s_new
            q_pair = q[r0:r0 + CHUNK, p * LANES:(p + 1) * LANES]
            o_h = lax.dot_general(q_pair, s_new.astype(BF16), (((1,), (1,)), ((), ())),
                                  preferred_element_type=F32)
            ms = jnp.mean(o_h * o_h, axis=-1, keepdims=True)
            zz = zbuf[r0:r0 + CHUNK, h * GLA_DV:(h + 1) * GLA_DV]
            res = o_h * lax.rsqrt(ms + EPS) * g_out * _silu(zz)
            o_ref[0, r0:r0 + CHUNK, h * GLA_DV:(h + 1) * GLA_DV] = res.astype(BF16)


def _proj_kernel(x_ref, mod_ref, g_ref, w_ref, wt_ref, ones_ref, qg_ref, kg_ref,
                 cw_ref, wgk_ref, bgk_ref, tri_ref, gg_ref,
                 dk_ref, dz_ref, qt_ref, vt_ref, og_ref, xbuf, zbuf, state):
    D = D_MODEL
    T = PROJ_TM

    @pl.when(pl.program_id(1) == 0)
    def _():
        xbuf[0:8, :] = jnp.zeros((8, GLA_CONV_WIDTH), F32)
        state[...] = jnp.zeros(state.shape, F32)

    x = x_ref[0]
    mod = mod_ref[0]
    shift = mod[:, 0:D]
    scale = mod[:, D:2 * D]
    ms = jnp.mean(x * x, axis=-1, keepdims=True)
    y = x * lax.rsqrt(ms + EPS) * g_ref[...]
    h = (y * (1.0 + scale) + shift).astype(BF16)

    def proj(a, n):
        return jnp.dot(h, w_ref[:, a:a + n], preferred_element_type=F32)

    def proj_t(a, n):
        return lax.dot_general(wt_ref[a:a + n, :], h, (((1,), (1,)), ((), ())),
                               preferred_element_type=F32)

    xbuf[8:8 + T, 0:512] = proj(0, 512)
    xbuf[8:8 + T, 512:1024] = proj(512, 512)
    glr = proj(2560, GATE_PAD).astype(BF16)
    zbuf[...] = proj(1024, 512)
    dk_raw = proj(1536, 512)
    gate = jnp.dot(glr, wgk_ref[...], preferred_element_type=F32) + bgk_ref[...]
    dz_ref[0] = proj(2048, 512).astype(BF16)
    dk_ref[0] = _group_rms(dk_raw, ones_ref[...], kg_ref[...]).astype(BF16)

    qt = proj_t(0, DIFF_QK)
    bcum = _gla_log_decay(gate, tri_ref)
    vt_ref[0, 0] = proj_t(DIFF_QK, DIFF_WIDTH).astype(BF16)
    for g in range(DIFF_QK // DIFF_DH):
        r0 = g * DIFF_DH
        t = qt[r0:r0 + DIFF_DH, :]
        ss = jnp.sum(t * t, axis=0, keepdims=True)
        t = t * lax.rsqrt(ss * (1.0 / DIFF_DH) + EPS) * qg_ref[r0:r0 + DIFF_DH, :]
        qt_ref[0, 0, r0:r0 + DIFF_DH, :] = (t * (DIFF_DH ** -0.5 * LOG2E)).astype(BF16)

    _gla_tile(xbuf, zbuf, state, bcum, cw_ref, gg_ref, og_ref)


def _in_projection(x, mod_l, norm_g, w_nat, w_t, ones_bd, qg_full, kg,
                   conv_w8, wgk_pad, bgk, tri, gla_g):
    B, S, D = x.shape
    tm = PROJ_TM
    NW = w_nat.shape[1]
    row = lambda b, i: (b, i, 0)
    tile = lambda b, i: (b, i, 0, 0)
    const = lambda b, i: (0, 0)
    once = pl.Buffered(1)
    return pl.pallas_call(
        _proj_kernel,
        out_shape=[jax.ShapeDtypeStruct((B, S, DIFF_QK), BF16),
                   jax.ShapeDtypeStruct((B, S, DIFF_WIDTH), BF16),
                   jax.ShapeDtypeStruct((B, S // tm, DIFF_QK, tm), BF16),
                   jax.ShapeDtypeStruct((B, S // tm, DIFF_WIDTH, tm), BF16),
                   jax.ShapeDtypeStruct((B, S, GLA_WIDTH), BF16)],
        grid=(B, S // tm),
        in_specs=[
            pl.BlockSpec((1, tm, D), row),
            pl.BlockSpec((1, 1, 3 * D), lambda b, i: (b, 0, 0)),
            pl.BlockSpec((1, D), const),
            pl.BlockSpec((D, NW), const, pipeline_mode=once),
            pl.BlockSpec((DIFF_QK + DIFF_WIDTH, D), const, pipeline_mode=once),
            pl.BlockSpec((512, 512), const, pipeline_mode=once),
            pl.BlockSpec((DIFF_QK, tm), const, pipeline_mode=once),
            pl.BlockSpec((1, 512), const),
            pl.BlockSpec((8, GLA_CONV_WIDTH), const),
            pl.BlockSpec((GATE_PAD, GLA_QK), const),
            pl.BlockSpec((1, GLA_QK), const),
            pl.BlockSpec((tm, tm), const, pipeline_mode=once),
            pl.BlockSpec((1, GLA_DV), const),
        ],
        out_specs=[pl.BlockSpec((1, tm, DIFF_QK), row),
                   pl.BlockSpec((1, tm, DIFF_WIDTH), row),
                   pl.BlockSpec((1, 1, DIFF_QK, tm), tile),
                   pl.BlockSpec((1, 1, DIFF_WIDTH, tm), tile),
                   pl.BlockSpec((1, tm, GLA_WIDTH), row)],
        scratch_shapes=[
            pltpu.VMEM((tm + 8, GLA_CONV_WIDTH), F32),
            pltpu.VMEM((tm, GLA_WIDTH), F32),
            pltpu.VMEM((GLA_HEADS, GLA_DV, LANES), F32),
        ],
        compiler_params=pltpu.CompilerParams(
            dimension_semantics=("arbitrary", "arbitrary"), vmem_limit_bytes=VMEM_LIMIT),
        name="in_proj_gla",
    )(x, mod_l, norm_g, w_nat, w_t, ones_bd, qg_full, kg, conv_w8, wgk_pad, bgk, tri, gla_g)


def _split_maps(qt_ref):
    qt = qt_ref[0, 0]
    row = lax.broadcasted_iota(jnp.int32, qt.shape, 0)
    zero = jnp.zeros_like(qt)
    return (jnp.where(row < DIFF_DH, qt, zero), jnp.where(row >= DIFF_DH, qt, zero))


def _chunk_mask():
    kc = lax.broadcasted_iota(jnp.int32, (ATT_T, ATT_T), 0) // CHUNK
    qc = lax.broadcasted_iota(jnp.int32, (ATT_T, ATT_T), 1) // CHUNK
    return kc <= qc


def _values_with_ones(vt_ref, j):
    return jnp.concatenate([vt_ref[0, j], jnp.ones((ATT_LROWS, ATT_T), BF16)], axis=0)


def _attn_finalize(acc_s, z_ref, lq1_ref, lk1_ref, lq2_ref, lk2_ref, li_ref, g_ref, o_ref):
    lam_init = li_ref[:, 0:1]
    lam = (jnp.exp(jnp.sum(lq1_ref[...] * lk1_ref[...], axis=-1, keepdims=True))
           - jnp.exp(jnp.sum(lq2_ref[...] * lk2_ref[...], axis=-1, keepdims=True))
           + lam_init)
    acc0, acc1 = acc_s[0], acc_s[1]
    out = (acc0[0:DIFF_DV] / acc0[DIFF_DV:DIFF_DV + 1]
           - lam * (acc1[0:DIFF_DV] / acc1[DIFF_DV:DIFF_DV + 1]))
    ms = jnp.mean(out * out, axis=0, keepdims=True)
    out = out * lax.rsqrt(ms + EPS) * g_ref[...] * (1.0 - lam_init)
    o_ref[0] = (out.T * _silu(z_ref[0].astype(F32))).astype(BF16)


def _attn_kernel_online(qt_ref, k_ref, vt_ref, z_ref, lq1_ref, lk1_ref, lq2_ref, lk2_ref,
                        li_ref, g_ref, o_ref, s_a, s_b, cm_a, cm_b, m_s, acc_s):
    T = ATT_T
    i = pl.program_id(2)
    qt_maps = _split_maps(qt_ref)

    m_s[...] = jnp.full(m_s.shape, -jnp.inf, F32)
    acc_s[...] = jnp.zeros(acc_s.shape, F32)

    def scores(j, s_ref, cm_ref):
        kj = k_ref[0, pl.ds(pl.multiple_of(j * T, T), T), :]
        for mp in range(2):
            s = jnp.dot(kj, qt_maps[mp], preferred_element_type=F32)
            s_ref[mp] = s
            cm_ref[mp] = jnp.max(s, axis=0, keepdims=True)

    def update(j, s_ref, cm_ref, mask):
        vtj = _values_with_ones(vt_ref, j)
        for mp in range(2):
            s = s_ref[mp]
            if mask is None:
                tile_max = cm_ref[mp]
            else:
                s = jnp.where(mask, s, -jnp.inf)
                tile_max = jnp.max(s, axis=0, keepdims=True)
            m_old = m_s[mp]
            m_new = jnp.maximum(m_old, tile_max)
            alpha = jnp.exp2(m_old - m_new)
            p = jnp.exp2(s - m_new)
            acc_s[mp] = alpha * acc_s[mp] + jnp.dot(vtj, p.astype(BF16),
                                                    preferred_element_type=F32)
            m_s[mp] = m_new

    scores(0, s_a, cm_a)

    def pair(t, carry):
        j = 2 * t
        scores(j + 1, s_b, cm_b)
        update(j, s_a, cm_a, None)
        scores(j + 2, s_a, cm_a)
        update(j + 1, s_b, cm_b, None)
        return carry

    lax.fori_loop(0, i // 2, pair, 0)
    mask = _chunk_mask()

    @pl.when(i % 2 == 0)
    def _():
        update(i, s_a, cm_a, mask)

    @pl.when(i % 2 == 1)
    def _():
        scores(i, s_b, cm_b)
        update(i - 1, s_a, cm_a, None)
        update(i, s_b, cm_b, mask)

    _attn_finalize(acc_s, z_ref, lq1_ref, lk1_ref, lq2_ref, lk2_ref, li_ref, g_ref, o_ref)


def _attn_kernel_bounded(qt_ref, k_ref, vt_ref, z_ref, lq1_ref, lk1_ref, lq2_ref, lk2_ref,
                         li_ref, g_ref, o_ref, acc_s):
    T, U = ATT_T, ATT_UNROLL
    i = pl.program_id(2)
    qt_maps = _split_maps(qt_ref)
    acc_s[...] = jnp.zeros(acc_s.shape, F32)

    def group(j0, n, mask):
        ps = ([], [])
        for u in range(n):
            kj = k_ref[0, pl.ds(pl.multiple_of((j0 + u) * T, T), T), :]
            for mp in range(2):
                p = jnp.exp2(jnp.dot(kj, qt_maps[mp], preferred_element_type=F32))
                if mask is not None and u == n - 1:
                    p = jnp.where(mask, p, 0.0)
                ps[mp].append(p.astype(BF16))
        vt = jnp.concatenate([_values_with_ones(vt_ref, j0 + u) for u in range(n)], axis=1)
        for mp in range(2):
            p_cat = jnp.concatenate(ps[mp], axis=0)
            acc_s[mp] += jnp.dot(vt, p_cat, preferred_element_type=F32)

    def body(t, carry):
        group(t * U, U, None)
        return carry

    lax.fori_loop(0, i // U, body, 0)
    mask = _chunk_mask()
    rem = i % U
    for r in range(U):
        @pl.when(rem == r)
        def _():
            group(i - r, r + 1, mask)

    _attn_finalize(acc_s, z_ref, lq1_ref, lk1_ref, lq2_ref, lk2_ref, li_ref, g_ref, o_ref)


def _diff_attention(bounded, qt, dk, vt, dz, lq1, lk1, lq2, lk2, lam_init_row, g_full):
    B, S, _ = dk.shape
    T = ATT_T
    H = DIFF_HEADS
    nt = S // T
    qmap = lambda b, h, i: (b, i, h)
    const = lambda b, h, i: (0, 0)
    acc = pltpu.VMEM((2, DIFF_DV + ATT_LROWS, T), F32)
    if bounded:
        body, scratch = _attn_kernel_bounded, [acc]
    else:
        body = _attn_kernel_online
        scratch = [pltpu.VMEM((2, T, T), F32), pltpu.VMEM((2, T, T), F32),
                   pltpu.VMEM((2, 1, T), F32), pltpu.VMEM((2, 1, T), F32),
                   pltpu.VMEM((2, 1, T), F32), acc]
    return pl.pallas_call(
        body,
        out_shape=jax.ShapeDtypeStruct((B, S, DIFF_WIDTH), BF16),
        grid=(B, H, nt),
        in_specs=[
            pl.BlockSpec((1, 1, 2 * DIFF_DH, T), lambda b, h, i: (b, i, h, 0)),
            pl.BlockSpec((1, S, LANES), lambda b, h, i: (b, 0, h)),
            pl.BlockSpec((1, nt, DIFF_DV, T), lambda b, h, i: (b, 0, h, 0)),
            pl.BlockSpec((1, T, LANES), qmap),
            pl.BlockSpec((1, DIFF_DH), const),
            pl.BlockSpec((1, DIFF_DH), const),
            pl.BlockSpec((1, DIFF_DH), const),
            pl.BlockSpec((1, DIFF_DH), const),
            pl.BlockSpec((1, LANES), const),
            pl.BlockSpec((DIFF_DV, T), const),
        ],
        out_specs=pl.BlockSpec((1, T, LANES), qmap),
        scratch_shapes=scratch,
        compiler_params=pltpu.CompilerParams(
            dimension_semantics=("arbitrary", "arbitrary", "arbitrary"),
            vmem_limit_bytes=VMEM_LIMIT),
        name="diff_attn_bounded" if bounded else "diff_attn_online",
    )(qt, dk, vt, dz, lq1, lk1, lq2, lk2, lam_init_row, g_full)


def _out_kernel(og_ref, od_ref, x_ref, mod_ref, w_ref, o_ref):
    D = D_MODEL
    gate = mod_ref[0][:, 2 * D:3 * D]
    y = (jnp.dot(og_ref[0], w_ref[0:GLA_WIDTH, :], preferred_element_type=F32)
         + jnp.dot(od_ref[0], w_ref[GLA_WIDTH:MIX_WIDTH, :], preferred_element_type=F32))
    o_ref[0] = x_ref[0] + gate * y


def _out_projection(o_gla, o_diff, x, mod_l, w_out):
    B, S, D = x.shape
    tm = OUT_TM
    row = lambda b, i: (b, i, 0)
    return pl.pallas_call(
        _out_kernel,
        out_shape=jax.ShapeDtypeStruct((B, S, D), F32),
        grid=(B, S // tm),
        in_specs=[
            pl.BlockSpec((1, tm, GLA_WIDTH), row),
            pl.BlockSpec((1, tm, DIFF_WIDTH), row),
            pl.BlockSpec((1, tm, D), row),
            pl.BlockSpec((1, 1, 3 * D), lambda b, i: (b, 0, 0)),
            pl.BlockSpec((MIX_WIDTH, D), lambda b, i: (0, 0), pipeline_mode=pl.Buffered(1)),
        ],
        out_specs=pl.BlockSpec((1, tm, D), row),
        compiler_params=pltpu.CompilerParams(
            dimension_semantics=("arbitrary", "arbitrary"), vmem_limit_bytes=VMEM_LIMIT),
        name="out_proj",
    )(o_gla, o_diff, x, mod_l, w_out)


def _split_w_in(w):
    idx = [0]
    for s in SPLIT_SIZES:
        idx.append(idx[-1] + s)
    gq, gk, gv, glr, gz, dq, dk, dv, dz = [w[:, idx[n]:idx[n + 1]] for n in range(9)]
    glr = jnp.pad(glr, ((0, 0), (0, GATE_PAD - GLA_GATE_RANK)))
    w_nat = jnp.concatenate([gq, gk, gv, gz, dk, dz, glr], axis=1).astype(BF16)
    w_t = jnp.concatenate([dq, dv], axis=1).T.astype(BF16)
    return w_nat, w_t


def kernel(x, c, w_ada, b_ada, norm_g, w_in, conv_w, w_gk, b_gk, gla_norm_g,
           qn_g, kn_g, lam_q1, lam_k1, lam_q2, lam_k2, diff_norm_g, w_out):
    B, S, D = x.shape
    mod = _modulation(c, w_ada, b_ada)

    r = jnp.arange(DIFF_QK) // DIFF_DH
    ones_bd = (r[:, None] == r[None, :]).astype(BF16)
    t = jnp.arange(PROJ_TM)
    tri = ((t[:, None] // CHUNK == t[None, :] // CHUNK)
           & (t[None, :] <= t[:, None])).astype(BF16)

    for l in range(DEPTH):
        mod_l = mod[l].reshape(B, 1, 3 * D)
        w_nat, w_t = _split_w_in(w_in[l])
        qg_full = jnp.broadcast_to(
            jnp.tile(qn_g[l], DIFF_QK // DIFF_DH)[:, None], (DIFF_QK, PROJ_TM))
        kg = jnp.tile(kn_g[l], DIFF_QK // DIFF_DH).reshape(1, DIFF_QK)
        conv_w8 = jnp.pad(conv_w[l], ((0, 8 - CONV_K), (0, 0)))
        wgk_pad = jnp.pad(w_gk[l], ((0, GATE_PAD - GLA_GATE_RANK), (0, 0))).astype(BF16)
        dk, dz, qt, vt, o_gla = _in_projection(
            x, mod_l, norm_g[l].reshape(1, D), w_nat, w_t, ones_bd, qg_full, kg,
            conv_w8, wgk_pad, b_gk[l].reshape(1, GLA_QK), tri, gla_norm_g[l].reshape(1, GLA_DV))

        lam_init = 0.8 - 0.6 * math.exp(-0.3 * l)
        score_bound = (DIFF_DH ** 0.5 * LOG2E * 1.01
                       * jnp.max(jnp.abs(qn_g[l])) * jnp.max(jnp.abs(kn_g[l])))
        attn_args = (
            qt, dk, vt, dz,
            lam_q1[l].reshape(1, DIFF_DH), lam_k1[l].reshape(1, DIFF_DH),
            lam_q2[l].reshape(1, DIFF_DH), lam_k2[l].reshape(1, DIFF_DH),
            jnp.full((1, LANES), lam_init, F32),
            jnp.broadcast_to(diff_norm_g[l][:, None], (DIFF_DV, ATT_T)))
        o_diff = lax.cond(score_bound <= ATT_SCORE_BOUND,
                          functools.partial(_diff_attention, True),
                          functools.partial(_diff_attention, False), *attn_args)

        x = _out_projection(o_gla, o_diff, x, mod_l, w_out[l].astype(BF16))
    return x
```

```python
import functools
import math

import jax
import jax.numpy as jnp
from jax import lax
from jax.experimental import pallas as pl
from jax.experimental.pallas import tpu as pltpu

D_MODEL = 1024
DEPTH = 4
CHUNK = 64
CONV_K = 4
EPS = 1e-6

GLA_HEADS = 4
GLA_DK = 64
GLA_DV = 128
GLA_QK = GLA_HEADS * GLA_DK
GLA_WIDTH = GLA_HEADS * GLA_DV
GLA_GATE_RANK = 16
GLA_GATE_TEMP = 16.0

DIFF_HEADS = 4
DIFF_DH = 64
DIFF_DV = 2 * DIFF_DH
DIFF_QK = DIFF_HEADS * 2 * DIFF_DH
DIFF_WIDTH = DIFF_HEADS * DIFF_DV

MIX_WIDTH = GLA_WIDTH + DIFF_WIDTH
GLA_CONV_WIDTH = 2 * GLA_QK + GLA_WIDTH
SPLIT_SIZES = (GLA_QK, GLA_QK, GLA_WIDTH, GLA_GATE_RANK, GLA_WIDTH,
               DIFF_QK, DIFF_QK, DIFF_WIDTH, DIFF_WIDTH)

LANES = 128
GATE_PAD = LANES
VMEM_LIMIT = 56 * 1024 * 1024

ATT_T = 512
ATT_QS = 4
ATT_UNROLL = 4
ATT_SCORE_BOUND = 32.0
ATT_LROWS = 16
PROJ_TM = ATT_T
OUT_TM = 512
MOD_TN = 512

F32 = jnp.float32
BF16 = jnp.bfloat16
LOG2E = math.log2(math.e)


def _sigmoid(x):
    return 1.0 / (1.0 + jnp.exp(-x))


def _silu(x):
    return x * _sigmoid(x)


def _mod_kernel(ct_ref, w_ref, b_ref, o_ref):
    ct = ct_ref[...]
    ca = _silu(ct)
    w = w_ref[0]
    rows = []
    for b in range(ct.shape[1]):
        rows.append(jnp.sum(w * ca[:, b:b + 1], axis=0, keepdims=True))
    o_ref[0] = jnp.concatenate(rows, axis=0) + b_ref[0]


def _modulation(c, w_ada, b_ada):
    B, D = c.shape
    L, _, N = w_ada.shape
    return pl.pallas_call(
        _mod_kernel,
        out_shape=jax.ShapeDtypeStruct((L, B, N), F32),
        grid=(L, N // MOD_TN),
        in_specs=[
            pl.BlockSpec((D, B), lambda l, n: (0, 0)),
            pl.BlockSpec((1, D, MOD_TN), lambda l, n: (l, 0, n)),
            pl.BlockSpec((1, 1, MOD_TN), lambda l, n: (l, 0, n)),
        ],
        out_specs=pl.BlockSpec((1, B, MOD_TN), lambda l, n: (l, 0, n)),
        compiler_params=pltpu.CompilerParams(
            dimension_semantics=("arbitrary", "arbitrary"), vmem_limit_bytes=VMEM_LIMIT),
        name="adaln_mod",
    )(c.T, w_ada, b_ada.reshape(L, 1, N))


def _group_rms(t, ones_bd, gain):
    ss = jnp.dot((t * t).astype(BF16), ones_bd, preferred_element_type=F32)
    return t * lax.rsqrt(ss * (1.0 / DIFF_DH) + EPS) * gain


def _gla_log_decay(gate, tri_ref):
    log_a = -(jnp.maximum(-gate, 0.0) + jnp.log1p(jnp.exp(-jnp.abs(gate))))
    log_a = log_a * (1.0 / GLA_GATE_TEMP)
    la_hi = log_a.astype(BF16)
    la_lo = (log_a - la_hi.astype(F32)).astype(BF16)
    tri = tri_ref[...]
    return (jnp.dot(tri, la_hi, preferred_element_type=F32)
            + jnp.dot(tri, la_lo, preferred_element_type=F32))


def _gla_tile(xbuf, zbuf, state, bcum, cw_ref, g_ref, o_ref):
    T = PROJ_TM
    n_chunks = T // CHUNK
    cw = cw_ref[...]
    conv = cw[0:1, :] * xbuf[pl.ds(8 - (CONV_K - 1), T), :]
    for j in range(1, CONV_K):
        conv = conv + cw[j:j + 1, :] * xbuf[pl.ds(8 - (CONV_K - 1) + j, T), :]
    xbuf[0:8, :] = xbuf[T:T + 8, :]
    act = _silu(conv)
    q = (act[:, 0:GLA_QK] * (GLA_DK ** -0.5)).astype(BF16)
    k = act[:, GLA_QK:2 * GLA_QK]
    v = act[:, 2 * GLA_QK:].astype(BF16)

    lane = lax.broadcasted_iota(jnp.int32, (CHUNK, LANES), 1)
    head_lanes = [(lane >= hh * GLA_DK) & (lane < (hh + 1) * GLA_DK) for hh in range(2)]

    decay, outer = [], []
    for c in range(n_chunks):
        r0 = c * CHUNK
        b_c = bcum[r0:r0 + CHUNK, :]
        b_end = b_c[CHUNK - 1:CHUNK, :]
        kdec = k[r0:r0 + CHUNK, :] * jnp.exp(b_end - b_c)
        decay.append(jnp.exp(b_end))
        for h in range(GLA_HEADS):
            p = h // 2
            kd_m = jnp.where(head_lanes[h % 2], kdec[:, p * LANES:(p + 1) * LANES], 0.0)
            v_h = v[r0:r0 + CHUNK, h * GLA_DV:(h + 1) * GLA_DV]
            outer.append(lax.dot_general(v_h, kd_m.astype(BF16), (((0,), (0,)), ((), ())),
                                         preferred_element_type=F32))

    g_out = g_ref[...]
    for c in range(n_chunks):
        r0 = c * CHUNK
        for h in range(GLA_HEADS):
            p = h // 2
            s_new = state[h] * decay[c][:, p * LANES:(p + 1) * LANES] + outer[c * GLA_HEADS + h]
            state[h] = s_new
            q_pair = q[r0:r0 + CHUNK, p * LANES:(p + 1) * LANES]
            o_h = lax.dot_general(q_pair, s_new.astype(BF16), (((1,), (1,)), ((), ())),
                                  preferred_element_type=F32)
            ms = jnp.mean(o_h * o_h, axis=-1, keepdims=True)
            zz = zbuf[r0:r0 + CHUNK, h * GLA_DV:(h + 1) * GLA_DV]
            res = o_h * lax.rsqrt(ms + EPS) * g_out * _silu(zz)
            o_ref[0, r0:r0 + CHUNK, h * GLA_DV:(h + 1) * GLA_DV] = res.astype(BF16)


def _proj_kernel(x_ref, mod_ref, g_ref, w_ref, wt_ref, ones_ref, qg_ref, kg_ref,
                 cw_ref, wgk_ref, bgk_ref, tri_ref, gg_ref,
                 dk_ref, dz_ref, qt_ref, vt_ref, og_ref, xbuf, zbuf, state):
    D = D_MODEL
    T = PROJ_TM

    @pl.when(pl.program_id(1) == 0)
    def _():
        xbuf[0:8, :] = jnp.zeros((8, GLA_CONV_WIDTH), F32)
        state[...] = jnp.zeros(state.shape, F32)

    x = x_ref[0]
    mod = mod_ref[0]
    shift = mod[:, 0:D]
    scale = mod[:, D:2 * D]
    ms = jnp.mean(x * x, axis=-1, keepdims=True)
    y = x * lax.rsqrt(ms + EPS) * g_ref[...]
    h = (y * (1.0 + scale) + shift).astype(BF16)

    def proj(a, n):
        return jnp.dot(h, w_ref[:, a:a + n], preferred_element_type=F32)

    def proj_t(a, n):
        return lax.dot_general(wt_ref[a:a + n, :], h, (((1,), (1,)), ((), ())),
                               preferred_element_type=F32)

    xbuf[8:8 + T, 0:512] = proj(0, 512)
    xbuf[8:8 + T, 512:1024] = proj(512, 512)
    glr = proj(2560, GATE_PAD).astype(BF16)
    zbuf[...] = proj(1024, 512)
    dk_raw = proj(1536, 512)
    gate = jnp.dot(glr, wgk_ref[...], preferred_element_type=F32) + bgk_ref[...]
    dz_ref[0] = proj(2048, 512).astype(BF16)
    dk_ref[0] = _group_rms(dk_raw, ones_ref[...], kg_ref[...]).astype(BF16)

    qt = proj_t(0, DIFF_QK)
    bcum = _gla_log_decay(gate, tri_ref)
    vt_ref[0, 0] = proj_t(DIFF_QK, DIFF_WIDTH).astype(BF16)
    for g in range(DIFF_QK // DIFF_DH):
        r0 = g * DIFF_DH
        t = qt[r0:r0 + DIFF_DH, :]
        ss = jnp.sum(t * t, axis=0, keepdims=True)
        t = t * lax.rsqrt(ss * (1.0 / DIFF_DH) + EPS) * qg_ref[r0:r0 + DIFF_DH, :]
        qt_ref[0, 0, r0:r0 + DIFF_DH, :] = (t * (DIFF_DH ** -0.5 * LOG2E)).astype(BF16)

    _gla_tile(xbuf, zbuf, state, bcum, cw_ref, gg_ref, og_ref)


def _in_projection(x, mod_l, norm_g, w_nat, w_t, ones_bd, qg_full, kg,
                   conv_w8, wgk_pad, bgk, tri, gla_g):
    B, S, D = x.shape
    tm = PROJ_TM
    NW = w_nat.shape[1]
    row = lambda b, i: (b, i, 0)
    tile = lambda b, i: (b, i, 0, 0)
    const = lambda b, i: (0, 0)
    once = pl.Buffered(1)
    return pl.pallas_call(
        _proj_kernel,
        out_shape=[jax.ShapeDtypeStruct((B, S, DIFF_QK), BF16),
                   jax.ShapeDtypeStruct((B, S, DIFF_WIDTH), BF16),
                   jax.ShapeDtypeStruct((B, S // tm, DIFF_QK, tm), BF16),
                   jax.ShapeDtypeStruct((B, S // tm, DIFF_WIDTH, tm), BF16),
                   jax.ShapeDtypeStruct((B, S, GLA_WIDTH), BF16)],
        grid=(B, S // tm),
        in_specs=[
            pl.BlockSpec((1, tm, D), row),
            pl.BlockSpec((1, 1, 3 * D), lambda b, i: (b, 0, 0)),
            pl.BlockSpec((1, D), const),
            pl.BlockSpec((D, NW), const, pipeline_mode=once),
            pl.BlockSpec((DIFF_QK + DIFF_WIDTH, D), const, pipeline_mode=once),
            pl.BlockSpec((512, 512), const, pipeline_mode=once),
            pl.BlockSpec((DIFF_QK, tm), const, pipeline_mode=once),
            pl.BlockSpec((1, 512), const),
            pl.BlockSpec((8, GLA_CONV_WIDTH), const),
            pl.BlockSpec((GATE_PAD, GLA_QK), const),
            pl.BlockSpec((1, GLA_QK), const),
            pl.BlockSpec((tm, tm), const, pipeline_mode=once),
            pl.BlockSpec((1, GLA_DV), const),
        ],
        out_specs=[pl.BlockSpec((1, tm, DIFF_QK), row),
                   pl.BlockSpec((1, tm, DIFF_WIDTH), row),
                   pl.BlockSpec((1, 1, DIFF_QK, tm), tile),
                   pl.BlockSpec((1, 1, DIFF_WIDTH, tm), tile),
                   pl.BlockSpec((1, tm, GLA_WIDTH), row)],
        scratch_shapes=[
            pltpu.VMEM((tm + 8, GLA_CONV_WIDTH), F32),
            pltpu.VMEM((tm, GLA_WIDTH), F32),
            pltpu.VMEM((GLA_HEADS, GLA_DV, LANES), F32),
        ],
        compiler_params=pltpu.CompilerParams(
            dimension_semantics=("arbitrary", "arbitrary"), vmem_limit_bytes=VMEM_LIMIT),
        name="in_proj_gla",
    )(x, mod_l, norm_g, w_nat, w_t, ones_bd, qg_full, kg, conv_w8, wgk_pad, bgk, tri, gla_g)


def _split_maps(qt_ref, s):
    qt = qt_ref[0, s]
    row = lax.broadcasted_iota(jnp.int32, qt.shape, 0)
    zero = jnp.zeros_like(qt)
    return (jnp.where(row < DIFF_DH, qt, zero), jnp.where(row >= DIFF_DH, qt, zero))


def _chunk_mask():
    kc = lax.broadcasted_iota(jnp.int32, (ATT_T, ATT_T), 0) // CHUNK
    qc = lax.broadcasted_iota(jnp.int32, (ATT_T, ATT_T), 1) // CHUNK
    return kc <= qc


def _values_with_ones(vt_ref, j):
    return jnp.concatenate([vt_ref[0, j], jnp.ones((ATT_LROWS, ATT_T), BF16)], axis=0)


def _attn_finalize(s, acc0, l0, acc1, l1, z_ref, lq1_ref, lk1_ref, lq2_ref, lk2_ref, li_ref,
                   g_ref, o_ref):
    lam_init = li_ref[:, 0:1]
    lam = (jnp.exp(jnp.sum(lq1_ref[...] * lk1_ref[...], axis=-1, keepdims=True))
           - jnp.exp(jnp.sum(lq2_ref[...] * lk2_ref[...], axis=-1, keepdims=True))
           + lam_init)
    out = acc0 / l0 - lam * (acc1 / l1)
    ms = jnp.mean(out * out, axis=0, keepdims=True)
    out = out * lax.rsqrt(ms + EPS) * g_ref[...] * (1.0 - lam_init)
    rows = pl.ds(pl.multiple_of(s * ATT_T, ATT_T), ATT_T)
    o_ref[0, rows, :] = (out.T * _silu(z_ref[0, rows, :].astype(F32))).astype(BF16)


def _attn_kernel_online(qt_ref, k_ref, vt_ref, z_ref, lq1_ref, lk1_ref, lq2_ref, lk2_ref,
                        li_ref, g_ref, o_ref, s_a, s_b, cm_a, cm_b, m_s, acc_s):
    T = ATT_T

    def tile(s, carry):
        i = pl.program_id(2) * ATT_QS + s
        qt_maps = _split_maps(qt_ref, s)

        m_s[...] = jnp.full(m_s.shape, -jnp.inf, F32)
        acc_s[...] = jnp.zeros(acc_s.shape, F32)

        def scores(j, s_ref, cm_ref):
            kj = k_ref[0, pl.ds(pl.multiple_of(j * T, T), T), :]
            for mp in range(2):
                s = jnp.dot(kj, qt_maps[mp], preferred_element_type=F32)
                s_ref[mp] = s
                cm_ref[mp] = jnp.max(s, axis=0, keepdims=True)

        def update(j, s_ref, cm_ref, mask):
            vtj = _values_with_ones(vt_ref, j)
            for mp in range(2):
                s = s_ref[mp]
                if mask is None:
                    tile_max = cm_ref[mp]
                else:
                    s = jnp.where(mask, s, -jnp.inf)
                    tile_max = jnp.max(s, axis=0, keepdims=True)
                m_old = m_s[mp]
                m_new = jnp.maximum(m_old, tile_max)
                alpha = jnp.exp2(m_old - m_new)
                p = jnp.exp2(s - m_new)
                acc_s[mp] = alpha * acc_s[mp] + jnp.dot(vtj, p.astype(BF16),
                                                        preferred_element_type=F32)
                m_s[mp] = m_new

        scores(0, s_a, cm_a)

        def pair(t, carry):
            j = 2 * t
            scores(j + 1, s_b, cm_b)
            update(j, s_a, cm_a, None)
            scores(j + 2, s_a, cm_a)
            update(j + 1, s_b, cm_b, None)
            return carry

        lax.fori_loop(0, i // 2, pair, 0)
        mask = _chunk_mask()

        @pl.when(i % 2 == 0)
        def _():
            update(i, s_a, cm_a, mask)

        @pl.when(i % 2 == 1)
        def _():
            scores(i, s_b, cm_b)
            update(i - 1, s_a, cm_a, None)
            update(i, s_b, cm_b, mask)

        acc0, acc1 = acc_s[0], acc_s[1]
        _attn_finalize(s, acc0[0:DIFF_DV], acc0[DIFF_DV:DIFF_DV + 1],
                       acc1[0:DIFF_DV], acc1[DIFF_DV:DIFF_DV + 1], z_ref,
                       lq1_ref, lk1_ref, lq2_ref, lk2_ref, li_ref, g_ref, o_ref)
        return carry

    lax.fori_loop(0, ATT_QS, tile, 0)


def _attn_kernel_bounded(qt_ref, k_ref, vt_ref, z_ref, lq1_ref, lk1_ref, lq2_ref, lk2_ref,
                         li_ref, g_ref, o_ref, acc_s, l_s):
    T, U = ATT_T, ATT_UNROLL

    def tile(s, carry):
        i = pl.program_id(2) * ATT_QS + s
        qt_maps = _split_maps(qt_ref, s)
        acc_s[...] = jnp.zeros(acc_s.shape, F32)
        l_s[...] = jnp.zeros(l_s.shape, F32)

        def group(j0, n, mask):
            ps = ([], [])
            lsum = [None, None]
            for u in range(n):
                kj = k_ref[0, pl.ds(pl.multiple_of((j0 + u) * T, T), T), :]
                for mp in range(2):
                    p = jnp.exp2(jnp.dot(kj, qt_maps[mp], preferred_element_type=F32))
                    if mask is not None and u == n - 1:
                        p = jnp.where(mask, p, 0.0)
                    psum = jnp.sum(p, axis=0, keepdims=True)
                    lsum[mp] = psum if lsum[mp] is None else lsum[mp] + psum
                    ps[mp].append(p.astype(BF16))
            vt = jnp.concatenate([vt_ref[0, j0 + u] for u in range(n)], axis=1)
            for mp in range(2):
                p_cat = jnp.concatenate(ps[mp], axis=0)
                acc_s[mp] += jnp.dot(vt, p_cat, preferred_element_type=F32)
                l_s[mp] += lsum[mp]

        def body(t, carry):
            group(t * U, U, None)
            return carry

        lax.fori_loop(0, i // U, body, 0)
        mask = _chunk_mask()
        rem = i % U
        for r in range(U):
            @pl.when(rem == r)
            def _():
                group(i - r, r + 1, mask)

        _attn_finalize(s, acc_s[0], l_s[0], acc_s[1], l_s[1],
                       z_ref, lq1_ref, lk1_ref, lq2_ref, lk2_ref, li_ref, g_ref, o_ref)
        return carry

    lax.fori_loop(0, ATT_QS, tile, 0)


def _diff_attention(bounded, qt, dk, vt, dz, lq1, lk1, lq2, lk2, lam_init_row, g_full):
    B, S, _ = dk.shape
    T = ATT_T
    H = DIFF_HEADS
    nt = S // T
    QS = ATT_QS
    qmap = lambda b, h, i: (b, i, h)
    const = lambda b, h, i: (0, 0)
    if bounded:
        body = _attn_kernel_bounded
        scratch = [pltpu.VMEM((2, DIFF_DV, T), F32), pltpu.VMEM((2, 1, T), F32)]
    else:
        body = _attn_kernel_online
        scratch = [pltpu.VMEM((2, T, T), F32), pltpu.VMEM((2, T, T), F32),
                   pltpu.VMEM((2, 1, T), F32), pltpu.VMEM((2, 1, T), F32),
                   pltpu.VMEM((2, 1, T), F32),
                   pltpu.VMEM((2, DIFF_DV + ATT_LROWS, T), F32)]
    return pl.pallas_call(
        body,
        out_shape=jax.ShapeDtypeStruct((B, S, DIFF_WIDTH), BF16),
        grid=(B, H, nt // QS),
        in_specs=[
            pl.BlockSpec((1, QS, 2 * DIFF_DH, T), lambda b, h, i: (b, i, h, 0)),
            pl.BlockSpec((1, S, LANES), lambda b, h, i: (b, 0, h)),
            pl.BlockSpec((1, nt, DIFF_DV, T), lambda b, h, i: (b, 0, h, 0)),
            pl.BlockSpec((1, QS * T, LANES), qmap),
            pl.BlockSpec((1, DIFF_DH), const),
            pl.BlockSpec((1, DIFF_DH), const),
            pl.BlockSpec((1, DIFF_DH), const),
            pl.BlockSpec((1, DIFF_DH), const),
            pl.BlockSpec((1, LANES), const),
            pl.BlockSpec((DIFF_DV, T), const),
        ],
        out_specs=pl.BlockSpec((1, QS * T, LANES), qmap),
        scratch_shapes=scratch,
        compiler_params=pltpu.CompilerParams(
            dimension_semantics=("arbitrary", "arbitrary", "arbitrary"),
            vmem_limit_bytes=VMEM_LIMIT),
        name="diff_attn_bounded" if bounded else "diff_attn_online",
    )(qt, dk, vt, dz, lq1, lk1, lq2, lk2, lam_init_row, g_full)


def _out_kernel(og_ref, od_ref, x_ref, mod_ref, w_ref, o_ref):
    D = D_MODEL
    gate = mod_ref[0][:, 2 * D:3 * D]
    y = (jnp.dot(og_ref[0], w_ref[0:GLA_WIDTH, :], preferred_element_type=F32)
         + jnp.dot(od_ref[0], w_ref[GLA_WIDTH:MIX_WIDTH, :], preferred_element_type=F32))
    o_ref[0] = x_ref[0] + gate * y


def _out_projection(o_gla, o_diff, x, mod_l, w_out):
    B, S, D = x.shape
    tm = OUT_TM
    row = lambda b, i: (b, i, 0)
    return pl.pallas_call(
        _out_kernel,
        out_shape=jax.ShapeDtypeStruct((B, S, D), F32),
        grid=(B, S // tm),
        in_specs=[
            pl.BlockSpec((1, tm, GLA_WIDTH), row),
            pl.BlockSpec((1, tm, DIFF_WIDTH), row),
            pl.BlockSpec((1, tm, D), row),
            pl.BlockSpec((1, 1, 3 * D), lambda b, i: (b, 0, 0)),
            pl.BlockSpec((MIX_WIDTH, D), lambda b, i: (0, 0), pipeline_mode=pl.Buffered(1)),
        ],
        out_specs=pl.BlockSpec((1, tm, D), row),
        compiler_params=pltpu.CompilerParams(
            dimension_semantics=("arbitrary", "arbitrary"), vmem_limit_bytes=VMEM_LIMIT),
        name="out_proj",
    )(o_gla, o_diff, x, mod_l, w_out)


def _split_w_in(w):
    idx = [0]
    for s in SPLIT_SIZES:
        idx.append(idx[-1] + s)
    gq, gk, gv, glr, gz, dq, dk, dv, dz = [w[:, idx[n]:idx[n + 1]] for n in range(9)]
    glr = jnp.pad(glr, ((0, 0), (0, GATE_PAD - GLA_GATE_RANK)))
    w_nat = jnp.concatenate([gq, gk, gv, gz, dk, dz, glr], axis=1).astype(BF16)
    w_t = jnp.concatenate([dq, dv], axis=1).T.astype(BF16)
    return w_nat, w_t


def kernel(x, c, w_ada, b_ada, norm_g, w_in, conv_w, w_gk, b_gk, gla_norm_g,
           qn_g, kn_g, lam_q1, lam_k1, lam_q2, lam_k2, diff_norm_g, w_out):
    B, S, D = x.shape
    mod = _modulation(c, w_ada, b_ada)

    r = jnp.arange(DIFF_QK) // DIFF_DH
    ones_bd = (r[:, None] == r[None, :]).astype(BF16)
    t = jnp.arange(PROJ_TM)
    tri = ((t[:, None] // CHUNK == t[None, :] // CHUNK)
           & (t[None, :] <= t[:, None])).astype(BF16)

    for l in range(DEPTH):
        mod_l = mod[l].reshape(B, 1, 3 * D)
        w_nat, w_t = _split_w_in(w_in[l])
        qg_full = jnp.broadcast_to(
            jnp.tile(qn_g[l], DIFF_QK // DIFF_DH)[:, None], (DIFF_QK, PROJ_TM))
        kg = jnp.tile(kn_g[l], DIFF_QK // DIFF_DH).reshape(1, DIFF_QK)
        conv_w8 = jnp.pad(conv_w[l], ((0, 8 - CONV_K), (0, 0)))
        wgk_pad = jnp.pad(w_gk[l], ((0, GATE_PAD - GLA_GATE_RANK), (0, 0))).astype(BF16)
        dk, dz, qt, vt, o_gla = _in_projection(
            x, mod_l, norm_g[l].reshape(1, D), w_nat, w_t, ones_bd, qg_full, kg,
            conv_w8, wgk_pad, b_gk[l].reshape(1, GLA_QK), tri, gla_norm_g[l].reshape(1, GLA_DV))

        lam_init = 0.8 - 0.6 * math.exp(-0.3 * l)
        score_bound = (DIFF_DH ** 0.5 * LOG2E * 1.01
                       * jnp.max(jnp.abs(qn_g[l])) * jnp.max(jnp.abs(kn_g[l])))
        attn_args = (
            qt, dk, vt, dz,
            lam_q1[l].reshape(1, DIFF_DH), lam_k1[l].reshape(1, DIFF_DH),
            lam_q2[l].reshape(1, DIFF_DH), lam_k2[l].reshape(1, DIFF_DH),
            jnp.full((1, LANES), lam_init, F32),
            jnp.broadcast_to(diff_norm_g[l][:, None], (DIFF_DV, ATT_T)))
        o_diff = lax.cond(score_bound <= ATT_SCORE_BOUND,
                          functools.partial(_diff_attention, True),
                          functools.partial(_diff_attention, False), *attn_args)

        x = _out_projection(o_gla, o_diff, x, mod_l, w_out[l].astype(BF16))
    return x
```

```python
import functools
import math

import jax
import jax.numpy as jnp
from jax import lax
from jax.experimental import pallas as pl
from jax.experimental.pallas import tpu as pltpu

D_MODEL = 1024
DEPTH = 4
CHUNK = 64
CONV_K = 4
EPS = 1e-6

GLA_HEADS = 4
GLA_DK = 64
GLA_DV = 128
GLA_QK = GLA_HEADS * GLA_DK
GLA_WIDTH = GLA_HEADS * GLA_DV
GLA_GATE_RANK = 16
GLA_GATE_TEMP = 16.0

DIFF_HEADS = 4
DIFF_DH = 64
DIFF_DV = 2 * DIFF_DH
DIFF_QK = DIFF_HEADS * 2 * DIFF_DH
DIFF_WIDTH = DIFF_HEADS * DIFF_DV

MIX_WIDTH = GLA_WIDTH + DIFF_WIDTH
GLA_CONV_WIDTH = 2 * GLA_QK + GLA_WIDTH
SPLIT_SIZES = (GLA_QK, GLA_QK, GLA_WIDTH, GLA_GATE_RANK, GLA_WIDTH,
               DIFF_QK, DIFF_QK, DIFF_WIDTH, DIFF_WIDTH)

LANES = 128
GATE_PAD = LANES
VMEM_LIMIT = 56 * 1024 * 1024

ATT_T = 512
ATT_QS = 4
ATT_UNROLL = 4
ATT_SCORE_BOUND = 32.0
ATT_LROWS = 16
PROJ_TM = ATT_T
OUT_TM = 512
MOD_TN = 512

F32 = jnp.float32
BF16 = jnp.bfloat16
LOG2E = math.log2(math.e)


def _sigmoid(x):
    return 1.0 / (1.0 + jnp.exp(-x))


def _silu(x):
    return x * _sigmoid(x)


def _mod_kernel(ct_ref, w_ref, b_ref, o_ref):
    ct = ct_ref[...]
    ca = _silu(ct)
    w = w_ref[0]
    rows = []
    for b in range(ct.shape[1]):
        rows.append(jnp.sum(w * ca[:, b:b + 1], axis=0, keepdims=True))
    o_ref[0] = jnp.concatenate(rows, axis=0) + b_ref[0]


def _modulation(c, w_ada, b_ada):
    B, D = c.shape
    L, _, N = w_ada.shape
    return pl.pallas_call(
        _mod_kernel,
        out_shape=jax.ShapeDtypeStruct((L, B, N), F32),
        grid=(L, N // MOD_TN),
        in_specs=[
            pl.BlockSpec((D, B), lambda l, n: (0, 0)),
            pl.BlockSpec((1, D, MOD_TN), lambda l, n: (l, 0, n)),
            pl.BlockSpec((1, 1, MOD_TN), lambda l, n: (l, 0, n)),
        ],
        out_specs=pl.BlockSpec((1, B, MOD_TN), lambda l, n: (l, 0, n)),
        compiler_params=pltpu.CompilerParams(
            dimension_semantics=("arbitrary", "arbitrary"), vmem_limit_bytes=VMEM_LIMIT),
        name="adaln_mod",
    )(c.T, w_ada, b_ada.reshape(L, 1, N))


def _gated_residual(x, og_ref, od_ref, mod_ref, w_ref):
    gate = mod_ref[0][:, 2 * D_MODEL:3 * D_MODEL]
    y = (jnp.dot(og_ref[0], w_ref[0:GLA_WIDTH, :], preferred_element_type=F32)
         + jnp.dot(od_ref[0], w_ref[GLA_WIDTH:MIX_WIDTH, :], preferred_element_type=F32))
    return x + gate * y


def _group_rms(t, ones_bd, gain):
    ss = jnp.dot((t * t).astype(BF16), ones_bd, preferred_element_type=F32)
    return t * lax.rsqrt(ss * (1.0 / DIFF_DH) + EPS) * gain


def _gla_log_decay(gate, tri_ref):
    log_a = -(jnp.maximum(-gate, 0.0) + jnp.log1p(jnp.exp(-jnp.abs(gate))))
    log_a = log_a * (1.0 / GLA_GATE_TEMP)
    la_hi = log_a.astype(BF16)
    la_lo = (log_a - la_hi.astype(F32)).astype(BF16)
    tri = tri_ref[...]
    return (jnp.dot(tri, la_hi, preferred_element_type=F32)
            + jnp.dot(tri, la_lo, preferred_element_type=F32))


def _gla_tile(xbuf, zbuf, state, bcum, cw_ref, g_ref, o_ref):
    T = PROJ_TM
    n_chunks = T // CHUNK
    cw = cw_ref[...]
    conv = cw[0:1, :] * xbuf[pl.ds(8 - (CONV_K - 1), T), :]
    for j in range(1, CONV_K):
        conv = conv + cw[j:j + 1, :] * xbuf[pl.ds(8 - (CONV_K - 1) + j, T), :]
    xbuf[0:8, :] = xbuf[T:T + 8, :]
    act = _silu(conv)
    q = (act[:, 0:GLA_QK] * (GLA_DK ** -0.5)).astype(BF16)
    k = act[:, GLA_QK:2 * GLA_QK]
    v = act[:, 2 * GLA_QK:].astype(BF16)

    lane = lax.broadcasted_iota(jnp.int32, (CHUNK, LANES), 1)
    head_lanes = [(lane >= hh * GLA_DK) & (lane < (hh + 1) * GLA_DK) for hh in range(2)]

    decay, outer = [], []
    for c in range(n_chunks):
        r0 = c * CHUNK
        b_c = bcum[r0:r0 + CHUNK, :]
        b_end = b_c[CHUNK - 1:CHUNK, :]
        kdec = k[r0:r0 + CHUNK, :] * jnp.exp(b_end - b_c)
        decay.append(jnp.exp(b_end))
        for h in range(GLA_HEADS):
            p = h // 2
            kd_m = jnp.where(head_lanes[h % 2], kdec[:, p * LANES:(p + 1) * LANES], 0.0)
            v_h = v[r0:r0 + CHUNK, h * GLA_DV:(h + 1) * GLA_DV]
            outer.append(lax.dot_general(v_h, kd_m.astype(BF16), (((0,), (0,)), ((), ())),
                                         preferred_element_type=F32))

    g_out = g_ref[...]
    for c in range(n_chunks):
        r0 = c * CHUNK
        for h in range(GLA_HEADS):
            p = h // 2
            s_new = state[h] * decay[c][:, p * LANES:(p + 1) * LANES] + outer[c * GLA_HEADS + h]
            state[h] = s_new
            q_pair = q[r0:r0 + CHUNK, p * LANES:(p + 1) * LANES]
            o_h = lax.dot_general(q_pair, s_new.astype(BF16), (((1,), (1,)), ((), ())),
                                  preferred_element_type=F32)
            ms = jnp.mean(o_h * o_h, axis=-1, keepdims=True)
            zz = zbuf[r0:r0 + CHUNK, h * GLA_DV:(h + 1) * GLA_DV]
            res = o_h * lax.rsqrt(ms + EPS) * g_out * _silu(zz)
            o_ref[0, r0:r0 + CHUNK, h * GLA_DV:(h + 1) * GLA_DV] = res.astype(BF16)


N_PROJ_INPUTS = 13
N_PREV_INPUTS = 4


def _proj_kernel(*refs, fuse_prev):
    if fuse_prev:
        ogp_ref, odp_ref, modp_ref, wout_ref = refs[1:1 + N_PREV_INPUTS]
        xo_ref = refs[N_PROJ_INPUTS + N_PREV_INPUTS]
        refs = (refs[:1] + refs[1 + N_PREV_INPUTS:N_PROJ_INPUTS + N_PREV_INPUTS]
                + refs[N_PROJ_INPUTS + N_PREV_INPUTS + 1:])
    (x_ref, mod_ref, g_ref, w_ref, wt_ref, ones_ref, qg_ref, kg_ref,
     cw_ref, wgk_ref, bgk_ref, tri_ref, gg_ref,
     dk_ref, dz_ref, qt_ref, vt_ref, og_ref, xbuf, zbuf, state) = refs
    D = D_MODEL
    T = PROJ_TM

    @pl.when(pl.program_id(1) == 0)
    def _():
        xbuf[0:8, :] = jnp.zeros((8, GLA_CONV_WIDTH), F32)
        state[...] = jnp.zeros(state.shape, F32)

    x = x_ref[0]
    if fuse_prev:
        x = _gated_residual(x, ogp_ref, odp_ref, modp_ref, wout_ref)
        xo_ref[0] = x
    mod = mod_ref[0]
    shift = mod[:, 0:D]
    scale = mod[:, D:2 * D]
    ms = jnp.mean(x * x, axis=-1, keepdims=True)
    y = x * lax.rsqrt(ms + EPS) * g_ref[...]
    h = (y * (1.0 + scale) + shift).astype(BF16)

    def proj(a, n):
        return jnp.dot(h, w_ref[:, a:a + n], preferred_element_type=F32)

    def proj_t(a, n):
        return lax.dot_general(wt_ref[a:a + n, :], h, (((1,), (1,)), ((), ())),
                               preferred_element_type=F32)

    xbuf[8:8 + T, 0:512] = proj(0, 512)
    xbuf[8:8 + T, 512:1024] = proj(512, 512)
    glr = proj(2560, GATE_PAD).astype(BF16)
    zbuf[...] = proj(1024, 512)
    dk_raw = proj(1536, 512)
    gate = jnp.dot(glr, wgk_ref[...], preferred_element_type=F32) + bgk_ref[...]
    dz_ref[0] = proj(2048, 512).astype(BF16)
    dk_ref[0] = _group_rms(dk_raw, ones_ref[...], kg_ref[...]).astype(BF16)

    qt = proj_t(0, DIFF_QK)
    bcum = _gla_log_decay(gate, tri_ref)
    vt_ref[0, 0] = proj_t(DIFF_QK, DIFF_WIDTH).astype(BF16)
    for g in range(DIFF_QK // DIFF_DH):
        r0 = g * DIFF_DH
        t = qt[r0:r0 + DIFF_DH, :]
        ss = jnp.sum(t * t, axis=0, keepdims=True)
        t = t * lax.rsqrt(ss * (1.0 / DIFF_DH) + EPS) * qg_ref[r0:r0 + DIFF_DH, :]
        qt_ref[0, 0, r0:r0 + DIFF_DH, :] = (t * (DIFF_DH ** -0.5 * LOG2E)).astype(BF16)

    _gla_tile(xbuf, zbuf, state, bcum, cw_ref, gg_ref, og_ref)


def _in_projection(x, prev, mod_l, norm_g, w_nat, w_t, ones_bd, qg_full, kg,
                   conv_w8, wgk_pad, bgk, tri, gla_g):
    B, S, D = x.shape
    tm = PROJ_TM
    NW = w_nat.shape[1]
    row = lambda b, i: (b, i, 0)
    tile = lambda b, i: (b, i, 0, 0)
    const = lambda b, i: (0, 0)
    per_batch = lambda b, i: (b, 0, 0)
    once = pl.Buffered(1)
    fuse_prev = prev is not None
    prev_specs = [pl.BlockSpec((1, tm, GLA_WIDTH), row),
                  pl.BlockSpec((1, tm, DIFF_WIDTH), row),
                  pl.BlockSpec((1, 1, 3 * D), per_batch),
                  pl.BlockSpec((MIX_WIDTH, D), const, pipeline_mode=once)] if fuse_prev else []
    x_out_shape = [jax.ShapeDtypeStruct((B, S, D), F32)] if fuse_prev else []
    x_out_spec = [pl.BlockSpec((1, tm, D), row)] if fuse_prev else []
    return pl.pallas_call(
        functools.partial(_proj_kernel, fuse_prev=fuse_prev),
        out_shape=x_out_shape + [
            jax.ShapeDtypeStruct((B, S, DIFF_QK), BF16),
            jax.ShapeDtypeStruct((B, S, DIFF_WIDTH), BF16),
            jax.ShapeDtypeStruct((B, S // tm, DIFF_QK, tm), BF16),
            jax.ShapeDtypeStruct((B, S // tm, DIFF_WIDTH, tm), BF16),
            jax.ShapeDtypeStruct((B, S, GLA_WIDTH), BF16)],
        grid=(B, S // tm),
        in_specs=[pl.BlockSpec((1, tm, D), row)] + prev_specs + [
            pl.BlockSpec((1, 1, 3 * D), per_batch),
            pl.BlockSpec((1, D), const),
            pl.BlockSpec((D, NW), const, pipeline_mode=once),
            pl.BlockSpec((DIFF_QK + DIFF_WIDTH, D), const, pipeline_mode=once),
            pl.BlockSpec((512, 512), const, pipeline_mode=once),
            pl.BlockSpec((DIFF_QK, tm), const, pipeline_mode=once),
            pl.BlockSpec((1, 512), const),
            pl.BlockSpec((8, GLA_CONV_WIDTH), const),
            pl.BlockSpec((GATE_PAD, GLA_QK), const),
            pl.BlockSpec((1, GLA_QK), const),
            pl.BlockSpec((tm, tm), const, pipeline_mode=once),
            pl.BlockSpec((1, GLA_DV), const),
        ],
        out_specs=x_out_spec + [
            pl.BlockSpec((1, tm, DIFF_QK), row),
            pl.BlockSpec((1, tm, DIFF_WIDTH), row),
            pl.BlockSpec((1, 1, DIFF_QK, tm), tile),
            pl.BlockSpec((1, 1, DIFF_WIDTH, tm), tile),
            pl.BlockSpec((1, tm, GLA_WIDTH), row)],
        scratch_shapes=[
            pltpu.VMEM((tm + 8, GLA_CONV_WIDTH), F32),
            pltpu.VMEM((tm, GLA_WIDTH), F32),
            pltpu.VMEM((GLA_HEADS, GLA_DV, LANES), F32),
        ],
        compiler_params=pltpu.CompilerParams(
            dimension_semantics=("arbitrary", "arbitrary"), vmem_limit_bytes=VMEM_LIMIT),
        name="out_in_proj_gla" if fuse_prev else "in_proj_gla",
    )(x, *(prev or ()), mod_l, norm_g, w_nat, w_t, ones_bd, qg_full, kg, conv_w8, wgk_pad,
      bgk, tri, gla_g)


def _split_maps(qt_ref, s):
    qt = qt_ref[0, s]
    row = lax.broadcasted_iota(jnp.int32, qt.shape, 0)
    zero = jnp.zeros_like(qt)
    return (jnp.where(row < DIFF_DH, qt, zero), jnp.where(row >= DIFF_DH, qt, zero))


def _chunk_mask():
    kc = lax.broadcasted_iota(jnp.int32, (ATT_T, ATT_T), 0) // CHUNK
    qc = lax.broadcasted_iota(jnp.int32, (ATT_T, ATT_T), 1) // CHUNK
    return kc <= qc


def _values_with_ones(vt_ref, j):
    return jnp.concatenate([vt_ref[0, j], jnp.ones((ATT_LROWS, ATT_T), BF16)], axis=0)


def _attn_finalize(s, acc0, l0, acc1, l1, z_ref, lq1_ref, lk1_ref, lq2_ref, lk2_ref, li_ref,
                   g_ref, o_ref):
    lam_init = li_ref[:, 0:1]
    lam = (jnp.exp(jnp.sum(lq1_ref[...] * lk1_ref[...], axis=-1, keepdims=True))
           - jnp.exp(jnp.sum(lq2_ref[...] * lk2_ref[...], axis=-1, keepdims=True))
           + lam_init)
    out = acc0 / l0 - lam * (acc1 / l1)
    ms = jnp.mean(out * out, axis=0, keepdims=True)
    out = out * lax.rsqrt(ms + EPS) * g_ref[...] * (1.0 - lam_init)
    rows = pl.ds(pl.multiple_of(s * ATT_T, ATT_T), ATT_T)
    o_ref[0, rows, :] = (out.T * _silu(z_ref[0, rows, :].astype(F32))).astype(BF16)


def _attn_kernel_online(qt_ref, k_ref, vt_ref, z_ref, lq1_ref, lk1_ref, lq2_ref, lk2_ref,
                        li_ref, g_ref, o_ref, s_a, s_b, cm_a, cm_b, m_s, acc_s):
    T = ATT_T

    def tile(s, carry):
        i = pl.program_id(2) * ATT_QS + s
        qt_maps = _split_maps(qt_ref, s)

        m_s[...] = jnp.full(m_s.shape, -jnp.inf, F32)
        acc_s[...] = jnp.zeros(acc_s.shape, F32)

        def scores(j, s_ref, cm_ref):
            kj = k_ref[0, pl.ds(pl.multiple_of(j * T, T), T), :]
            for mp in range(2):
                s = jnp.dot(kj, qt_maps[mp], preferred_element_type=F32)
                s_ref[mp] = s
                cm_ref[mp] = jnp.max(s, axis=0, keepdims=True)

        def update(j, s_ref, cm_ref, mask):
            vtj = _values_with_ones(vt_ref, j)
            for mp in range(2):
                s = s_ref[mp]
                if mask is None:
                    tile_max = cm_ref[mp]
                else:
                    s = jnp.where(mask, s, -jnp.inf)
                    tile_max = jnp.max(s, axis=0, keepdims=True)
                m_old = m_s[mp]
                m_new = jnp.maximum(m_old, tile_max)
                alpha = jnp.exp2(m_old - m_new)
                p = jnp.exp2(s - m_new)
                acc_s[mp] = alpha * acc_s[mp] + jnp.dot(vtj, p.astype(BF16),
                                                        preferred_element_type=F32)
                m_s[mp] = m_new

        scores(0, s_a, cm_a)

        def pair(t, carry):
            j = 2 * t
            scores(j + 1, s_b, cm_b)
            update(j, s_a, cm_a, None)
            scores(j + 2, s_a, cm_a)
            update(j + 1, s_b, cm_b, None)
            return carry

        lax.fori_loop(0, i // 2, pair, 0)
        mask = _chunk_mask()

        @pl.when(i % 2 == 0)
        def _():
            update(i, s_a, cm_a, mask)

        @pl.when(i % 2 == 1)
        def _():
            scores(i, s_b, cm_b)
            update(i - 1, s_a, cm_a, None)
            update(i, s_b, cm_b, mask)

        acc0, acc1 = acc_s[0], acc_s[1]
        _attn_finalize(s, acc0[0:DIFF_DV], acc0[DIFF_DV:DIFF_DV + 1],
                       acc1[0:DIFF_DV], acc1[DIFF_DV:DIFF_DV + 1], z_ref,
                       lq1_ref, lk1_ref, lq2_ref, lk2_ref, li_ref, g_ref, o_ref)
        return carry

    lax.fori_loop(0, ATT_QS, tile, 0)


def _attn_kernel_bounded(qt_ref, k_ref, vt_ref, z_ref, lq1_ref, lk1_ref, lq2_ref, lk2_ref,
                         li_ref, g_ref, o_ref, acc_s, l_s):
    T, U = ATT_T, ATT_UNROLL

    def tile(s, carry):
        i = pl.program_id(2) * ATT_QS + s
        qt_maps = _split_maps(qt_ref, s)
        acc_s[...] = jnp.zeros(acc_s.shape, F32)
        l_s[...] = jnp.zeros(l_s.shape, F32)

        def group(j0, n, mask):
            ps = ([], [])
            lsum = [None, None]
            for u in range(n):
                kj = k_ref[0, pl.ds(pl.multiple_of((j0 + u) * T, T), T), :]
                for mp in range(2):
                    p = jnp.exp2(jnp.dot(kj, qt_maps[mp], preferred_element_type=F32))
                    if mask is not None and u == n - 1:
                        p = jnp.where(mask, p, 0.0)
                    psum = jnp.sum(p, axis=0, keepdims=True)
                    lsum[mp] = psum if lsum[mp] is None else lsum[mp] + psum
                    ps[mp].append(p.astype(BF16))
            vt = jnp.concatenate([vt_ref[0, j0 + u] for u in range(n)], axis=1)
            for mp in range(2):
                p_cat = jnp.concatenate(ps[mp], axis=0)
                acc_s[mp] += jnp.dot(vt, p_cat, preferred_element_type=F32)
                l_s[mp] += lsum[mp]

        def body(t, carry):
            group(t * U, U, None)
            return carry

        lax.fori_loop(0, i // U, body, 0)
        mask = _chunk_mask()
        rem = i % U
        for r in range(U):
            @pl.when(rem == r)
            def _():
                group(i - r, r + 1, mask)

        _attn_finalize(s, acc_s[0], l_s[0], acc_s[1], l_s[1],
                       z_ref, lq1_ref, lk1_ref, lq2_ref, lk2_ref, li_ref, g_ref, o_ref)
        return carry

    lax.fori_loop(0, ATT_QS, tile, 0)


def _diff_attention(bounded, qt, dk, vt, dz, lq1, lk1, lq2, lk2, lam_init_row, g_full):
    B, S, _ = dk.shape
    T = ATT_T
    H = DIFF_HEADS
    nt = S // T
    QS = ATT_QS
    qmap = lambda b, h, i: (b, i, h)
    const = lambda b, h, i: (0, 0)
    if bounded:
        body = _attn_kernel_bounded
        scratch = [pltpu.VMEM((2, DIFF_DV, T), F32), pltpu.VMEM((2, 1, T), F32)]
    else:
        body = _attn_kernel_online
        scratch = [pltpu.VMEM((2, T, T), F32), pltpu.VMEM((2, T, T), F32),
                   pltpu.VMEM((2, 1, T), F32), pltpu.VMEM((2, 1, T), F32),
                   pltpu.VMEM((2, 1, T), F32),
                   pltpu.VMEM((2, DIFF_DV + ATT_LROWS, T), F32)]
    return pl.pallas_call(
        body,
        out_shape=jax.ShapeDtypeStruct((B, S, DIFF_WIDTH), BF16),
        grid=(B, H, nt // QS),
        in_specs=[
            pl.BlockSpec((1, QS, 2 * DIFF_DH, T), lambda b, h, i: (b, i, h, 0)),
            pl.BlockSpec((1, S, LANES), lambda b, h, i: (b, 0, h)),
            pl.BlockSpec((1, nt, DIFF_DV, T), lambda b, h, i: (b, 0, h, 0)),
            pl.BlockSpec((1, QS * T, LANES), qmap),
            pl.BlockSpec((1, DIFF_DH), const),
            pl.BlockSpec((1, DIFF_DH), const),
            pl.BlockSpec((1, DIFF_DH), const),
            pl.BlockSpec((1, DIFF_DH), const),
            pl.BlockSpec((1, LANES), const),
            pl.BlockSpec((DIFF_DV, T), const),
        ],
        out_specs=pl.BlockSpec((1, QS * T, LANES), qmap),
        scratch_shapes=scratch,
        compiler_params=pltpu.CompilerParams(
            dimension_semantics=("arbitrary", "arbitrary", "arbitrary"),
            vmem_limit_bytes=VMEM_LIMIT),
        name="diff_attn_bounded" if bounded else "diff_attn_online",
    )(qt, dk, vt, dz, lq1, lk1, lq2, lk2, lam_init_row, g_full)


def _out_kernel(x_ref, og_ref, od_ref, mod_ref, w_ref, o_ref):
    o_ref[0] = _gated_residual(x_ref[0], og_ref, od_ref, mod_ref, w_ref)


def _out_projection(x, o_gla, o_diff, mod_l, w_out):
    B, S, D = x.shape
    tm = OUT_TM
    row = lambda b, i: (b, i, 0)
    return pl.pallas_call(
        _out_kernel,
        out_shape=jax.ShapeDtypeStruct((B, S, D), F32),
        grid=(B, S // tm),
        in_specs=[
            pl.BlockSpec((1, tm, D), row),
            pl.BlockSpec((1, tm, GLA_WIDTH), row),
            pl.BlockSpec((1, tm, DIFF_WIDTH), row),
            pl.BlockSpec((1, 1, 3 * D), lambda b, i: (b, 0, 0)),
            pl.BlockSpec((MIX_WIDTH, D), lambda b, i: (0, 0), pipeline_mode=pl.Buffered(1)),
        ],
        out_specs=pl.BlockSpec((1, tm, D), row),
        compiler_params=pltpu.CompilerParams(
            dimension_semantics=("arbitrary", "arbitrary"), vmem_limit_bytes=VMEM_LIMIT),
        name="out_proj",
    )(x, o_gla, o_diff, mod_l, w_out)


def _split_w_in(w):
    idx = [0]
    for s in SPLIT_SIZES:
        idx.append(idx[-1] + s)
    gq, gk, gv, glr, gz, dq, dk, dv, dz = [w[:, idx[n]:idx[n + 1]] for n in range(9)]
    glr = jnp.pad(glr, ((0, 0), (0, GATE_PAD - GLA_GATE_RANK)))
    w_nat = jnp.concatenate([gq, gk, gv, gz, dk, dz, glr], axis=1).astype(BF16)
    w_t = jnp.concatenate([dq, dv], axis=1).T.astype(BF16)
    return w_nat, w_t


def kernel(x, c, w_ada, b_ada, norm_g, w_in, conv_w, w_gk, b_gk, gla_norm_g,
           qn_g, kn_g, lam_q1, lam_k1, lam_q2, lam_k2, diff_norm_g, w_out):
    B, S, D = x.shape
    mod = _modulation(c, w_ada, b_ada)

    r = jnp.arange(DIFF_QK) // DIFF_DH
    ones_bd = (r[:, None] == r[None, :]).astype(BF16)
    t = jnp.arange(PROJ_TM)
    tri = ((t[:, None] // CHUNK == t[None, :] // CHUNK)
           & (t[None, :] <= t[:, None])).astype(BF16)

    prev = None
    for l in range(DEPTH):
        mod_l = mod[l].reshape(B, 1, 3 * D)
        w_nat, w_t = _split_w_in(w_in[l])
        qg_full = jnp.broadcast_to(
            jnp.tile(qn_g[l], DIFF_QK // DIFF_DH)[:, None], (DIFF_QK, PROJ_TM))
        kg = jnp.tile(kn_g[l], DIFF_QK // DIFF_DH).reshape(1, DIFF_QK)
        conv_w8 = jnp.pad(conv_w[l], ((0, 8 - CONV_K), (0, 0)))
        wgk_pad = jnp.pad(w_gk[l], ((0, GATE_PAD - GLA_GATE_RANK), (0, 0))).astype(BF16)
        outs = _in_projection(
            x, prev, mod_l, norm_g[l].reshape(1, D), w_nat, w_t, ones_bd, qg_full, kg,
            conv_w8, wgk_pad, b_gk[l].reshape(1, GLA_QK), tri, gla_norm_g[l].reshape(1, GLA_DV))
        if prev is not None:
            x, outs = outs[0], outs[1:]
        dk, dz, qt, vt, o_gla = outs

        lam_init = 0.8 - 0.6 * math.exp(-0.3 * l)
        score_bound = (DIFF_DH ** 0.5 * LOG2E * 1.01
                       * jnp.max(jnp.abs(qn_g[l])) * jnp.max(jnp.abs(kn_g[l])))
        attn_args = (
            qt, dk, vt, dz,
            lam_q1[l].reshape(1, DIFF_DH), lam_k1[l].reshape(1, DIFF_DH),
            lam_q2[l].reshape(1, DIFF_DH), lam_k2[l].reshape(1, DIFF_DH),
            jnp.full((1, LANES), lam_init, F32),
            jnp.broadcast_to(diff_norm_g[l][:, None], (DIFF_DV, ATT_T)))
        o_diff = lax.cond(score_bound <= ATT_SCORE_BOUND,
                          functools.partial(_diff_attention, True),
                          functools.partial(_diff_attention, False), *attn_args)

        prev = (o_gla, o_diff, mod_l, w_out[l].astype(BF16))
    return _out_projection(x, *prev)
```

```python
import functools
import math

import jax
import jax.numpy as jnp
from jax import lax
from jax.experimental import pallas as pl
from jax.experimental.pallas import tpu as pltpu

D_MODEL = 1024
DEPTH = 4
CHUNK = 64
CONV_K = 4
EPS = 1e-6

GLA_HEADS = 4
GLA_DK = 64
GLA_DV = 128
GLA_QK = GLA_HEADS * GLA_DK
GLA_WIDTH = GLA_HEADS * GLA_DV
GLA_GATE_RANK = 16
GLA_GATE_TEMP = 16.0

DIFF_HEADS = 4
DIFF_DH = 64
DIFF_DV = 2 * DIFF_DH
DIFF_QK = DIFF_HEADS * 2 * DIFF_DH
DIFF_WIDTH = DIFF_HEADS * DIFF_DV

MIX_WIDTH = GLA_WIDTH + DIFF_WIDTH
GLA_CONV_WIDTH = 2 * GLA_QK + GLA_WIDTH
SPLIT_SIZES = (GLA_QK, GLA_QK, GLA_WIDTH, GLA_GATE_RANK, GLA_WIDTH,
               DIFF_QK, DIFF_QK, DIFF_WIDTH, DIFF_WIDTH)

LANES = 128
GATE_PAD = LANES
VMEM_LIMIT = 56 * 1024 * 1024

ATT_T = 512
ATT_QS = 4
ATT_UNROLL = 8
ATT_SCORE_BOUND = 32.0
ATT_LROWS = 16
PROJ_TM = ATT_T
OUT_TM = 512
MOD_TN = 512

F32 = jnp.float32
BF16 = jnp.bfloat16
LOG2E = math.log2(math.e)


def _sigmoid(x):
    return 1.0 / (1.0 + jnp.exp(-x))


def _silu(x):
    return x * _sigmoid(x)


def _mod_kernel(ct_ref, w_ref, b_ref, o_ref):
    ct = ct_ref[...]
    ca = _silu(ct)
    w = w_ref[0]
    rows = []
    for b in range(ct.shape[1]):
        rows.append(jnp.sum(w * ca[:, b:b + 1], axis=0, keepdims=True))
    o_ref[0] = jnp.concatenate(rows, axis=0) + b_ref[0]


def _modulation(c, w_ada, b_ada):
    B, D = c.shape
    L, _, N = w_ada.shape
    return pl.pallas_call(
        _mod_kernel,
        out_shape=jax.ShapeDtypeStruct((L, B, N), F32),
        grid=(L, N // MOD_TN),
        in_specs=[
            pl.BlockSpec((D, B), lambda l, n: (0, 0)),
            pl.BlockSpec((1, D, MOD_TN), lambda l, n: (l, 0, n)),
            pl.BlockSpec((1, 1, MOD_TN), lambda l, n: (l, 0, n)),
        ],
        out_specs=pl.BlockSpec((1, B, MOD_TN), lambda l, n: (l, 0, n)),
        compiler_params=pltpu.CompilerParams(
            dimension_semantics=("arbitrary", "arbitrary"), vmem_limit_bytes=VMEM_LIMIT),
        name="adaln_mod",
    )(c.T, w_ada, b_ada.reshape(L, 1, N))


def _gated_residual(x, og_ref, od_ref, mod_ref, w_ref):
    gate = mod_ref[0][:, 2 * D_MODEL:3 * D_MODEL]
    y = (jnp.dot(og_ref[0], w_ref[0:GLA_WIDTH, :], preferred_element_type=F32)
         + jnp.dot(od_ref[0], w_ref[GLA_WIDTH:MIX_WIDTH, :], preferred_element_type=F32))
    return x + gate * y


def _group_rms(t, ones_bd, gain):
    ss = jnp.dot((t * t).astype(BF16), ones_bd, preferred_element_type=F32)
    return t * lax.rsqrt(ss * (1.0 / DIFF_DH) + EPS) * gain


def _gla_log_decay(gate, tri_ref):
    log_a = -(jnp.maximum(-gate, 0.0) + jnp.log1p(jnp.exp(-jnp.abs(gate))))
    log_a = log_a * (1.0 / GLA_GATE_TEMP)
    la_hi = log_a.astype(BF16)
    la_lo = (log_a - la_hi.astype(F32)).astype(BF16)
    tri = tri_ref[...]
    return (jnp.dot(tri, la_hi, preferred_element_type=F32)
            + jnp.dot(tri, la_lo, preferred_element_type=F32))


def _gla_tile(xbuf, zbuf, state, bcum, cw_ref, g_ref, o_ref):
    T = PROJ_TM
    n_chunks = T // CHUNK
    cw = cw_ref[...]
    conv = cw[0:1, :] * xbuf[pl.ds(8 - (CONV_K - 1), T), :]
    for j in range(1, CONV_K):
        conv = conv + cw[j:j + 1, :] * xbuf[pl.ds(8 - (CONV_K - 1) + j, T), :]
    xbuf[0:8, :] = xbuf[T:T + 8, :]
    act = _silu(conv)
    q = (act[:, 0:GLA_QK] * (GLA_DK ** -0.5)).astype(BF16)
    k = act[:, GLA_QK:2 * GLA_QK]
    v = act[:, 2 * GLA_QK:].astype(BF16)

    lane = lax.broadcasted_iota(jnp.int32, (CHUNK, LANES), 1)
    head_lanes = [(lane >= hh * GLA_DK) & (lane < (hh + 1) * GLA_DK) for hh in range(2)]

    decay, outer = [], []
    for c in range(n_chunks):
        r0 = c * CHUNK
        b_c = bcum[r0:r0 + CHUNK, :]
        b_end = b_c[CHUNK - 1:CHUNK, :]
        kdec = k[r0:r0 + CHUNK, :] * jnp.exp(b_end - b_c)
        decay.append(jnp.exp(b_end))
        for h in range(GLA_HEADS):
            p = h // 2
            kd_m = jnp.where(head_lanes[h % 2], kdec[:, p * LANES:(p + 1) * LANES], 0.0)
            v_h = v[r0:r0 + CHUNK, h * GLA_DV:(h + 1) * GLA_DV]
            outer.append(lax.dot_general(v_h, kd_m.astype(BF16), (((0,), (0,)), ((), ())),
                                         preferred_element_type=F32))

    g_out = g_ref[...]
    for c in range(n_chunks):
        r0 = c * CHUNK
        for h in range(GLA_HEADS):
            p = h // 2
            s_new = state[h] * decay[c][:, p * LANES:(p + 1) * LANES] + outer[c * GLA_HEADS + h]
            state[h] = s_new
            q_pair = q[r0:r0 + CHUNK, p * LANES:(p + 1) * LANES]
            o_h = lax.dot_general(q_pair, s_new.astype(BF16), (((1,), (1,)), ((), ())),
                                  preferred_element_type=F32)
            ms = jnp.mean(o_h * o_h, axis=-1, keepdims=True)
            zz = zbuf[r0:r0 + CHUNK, h * GLA_DV:(h + 1) * GLA_DV]
            res = o_h * lax.rsqrt(ms + EPS) * g_out * _silu(zz)
            o_ref[0, r0:r0 + CHUNK, h * GLA_DV:(h + 1) * GLA_DV] = res.astype(BF16)


N_PROJ_INPUTS = 13
N_PREV_INPUTS = 4


def _proj_kernel(*refs, fuse_prev):
    if fuse_prev:
        ogp_ref, odp_ref, modp_ref, wout_ref = refs[1:1 + N_PREV_INPUTS]
        xo_ref = refs[N_PROJ_INPUTS + N_PREV_INPUTS]
        refs = (refs[:1] + refs[1 + N_PREV_INPUTS:N_PROJ_INPUTS + N_PREV_INPUTS]
                + refs[N_PROJ_INPUTS + N_PREV_INPUTS + 1:])
    (x_ref, mod_ref, g_ref, w_ref, wt_ref, ones_ref, qg_ref, kg_ref,
     cw_ref, wgk_ref, bgk_ref, tri_ref, gg_ref,
     dk_ref, dz_ref, qt_ref, vt_ref, og_ref, xbuf, zbuf, state) = refs
    D = D_MODEL
    T = PROJ_TM

    @pl.when(pl.program_id(1) == 0)
    def _():
        xbuf[0:8, :] = jnp.zeros((8, GLA_CONV_WIDTH), F32)
        state[...] = jnp.zeros(state.shape, F32)

    x = x_ref[0]
    if fuse_prev:
        x = _gated_residual(x, ogp_ref, odp_ref, modp_ref, wout_ref)
        xo_ref[0] = x
    mod = mod_ref[0]
    shift = mod[:, 0:D]
    scale = mod[:, D:2 * D]
    ms = jnp.mean(x * x, axis=-1, keepdims=True)
    y = x * lax.rsqrt(ms + EPS) * g_ref[...]
    h = (y * (1.0 + scale) + shift).astype(BF16)

    def proj(a, n):
        return jnp.dot(h, w_ref[:, a:a + n], preferred_element_type=F32)

    def proj_t(a, n):
        return lax.dot_general(wt_ref[a:a + n, :], h, (((1,), (1,)), ((), ())),
                               preferred_element_type=F32)

    xbuf[8:8 + T, 0:512] = proj(0, 512)
    xbuf[8:8 + T, 512:1024] = proj(512, 512)
    glr = proj(2560, GATE_PAD).astype(BF16)
    zbuf[...] = proj(1024, 512)
    dk_raw = proj(1536, 512)
    gate = jnp.dot(glr, wgk_ref[...], preferred_element_type=F32) + bgk_ref[...]
    dz_ref[0] = proj(2048, 512).astype(BF16)
    dk_ref[0] = _group_rms(dk_raw, ones_ref[...], kg_ref[...]).astype(BF16)

    qt = proj_t(0, DIFF_QK)
    bcum = _gla_log_decay(gate, tri_ref)
    vt_ref[0, 0] = proj_t(DIFF_QK, DIFF_WIDTH).astype(BF16)
    for g in range(DIFF_QK // DIFF_DH):
        r0 = g * DIFF_DH
        t = qt[r0:r0 + DIFF_DH, :]
        ss = jnp.sum(t * t, axis=0, keepdims=True)
        t = t * lax.rsqrt(ss * (1.0 / DIFF_DH) + EPS) * qg_ref[r0:r0 + DIFF_DH, :]
        qt_ref[0, 0, r0:r0 + DIFF_DH, :] = (t * (DIFF_DH ** -0.5 * LOG2E)).astype(BF16)

    _gla_tile(xbuf, zbuf, state, bcum, cw_ref, gg_ref, og_ref)


def _in_projection(x, prev, mod_l, norm_g, w_nat, w_t, ones_bd, qg_full, kg,
                   conv_w8, wgk_pad, bgk, tri, gla_g):
    B, S, D = x.shape
    tm = PROJ_TM
    NW = w_nat.shape[1]
    row = lambda b, i: (b, i, 0)
    tile = lambda b, i: (b, i, 0, 0)
    const = lambda b, i: (0, 0)
    per_batch = lambda b, i: (b, 0, 0)
    once = pl.Buffered(1)
    fuse_prev = prev is not None
    prev_specs = [pl.BlockSpec((1, tm, GLA_WIDTH), row),
                  pl.BlockSpec((1, tm, DIFF_WIDTH), row),
                  pl.BlockSpec((1, 1, 3 * D), per_batch),
                  pl.BlockSpec((MIX_WIDTH, D), const, pipeline_mode=once)] if fuse_prev else []
    x_out_shape = [jax.ShapeDtypeStruct((B, S, D), F32)] if fuse_prev else []
    x_out_spec = [pl.BlockSpec((1, tm, D), row)] if fuse_prev else []
    return pl.pallas_call(
        functools.partial(_proj_kernel, fuse_prev=fuse_prev),
        out_shape=x_out_shape + [
            jax.ShapeDtypeStruct((B, S, DIFF_QK), BF16),
            jax.ShapeDtypeStruct((B, S, DIFF_WIDTH), BF16),
            jax.ShapeDtypeStruct((B, S // tm, DIFF_QK, tm), BF16),
            jax.ShapeDtypeStruct((B, S // tm, DIFF_WIDTH, tm), BF16),
            jax.ShapeDtypeStruct((B, S, GLA_WIDTH), BF16)],
        grid=(B, S // tm),
        in_specs=[pl.BlockSpec((1, tm, D), row)] + prev_specs + [
            pl.BlockSpec((1, 1, 3 * D), per_batch),
            pl.BlockSpec((1, D), const),
            pl.BlockSpec((D, NW), const, pipeline_mode=once),
            pl.BlockSpec((DIFF_QK + DIFF_WIDTH, D), const, pipeline_mode=once),
            pl.BlockSpec((512, 512), const, pipeline_mode=once),
            pl.BlockSpec((DIFF_QK, tm), const, pipeline_mode=once),
            pl.BlockSpec((1, 512), const),
            pl.BlockSpec((8, GLA_CONV_WIDTH), const),
            pl.BlockSpec((GATE_PAD, GLA_QK), const),
            pl.BlockSpec((1, GLA_QK), const),
            pl.BlockSpec((tm, tm), const, pipeline_mode=once),
            pl.BlockSpec((1, GLA_DV), const),
        ],
        out_specs=x_out_spec + [
            pl.BlockSpec((1, tm, DIFF_QK), row),
            pl.BlockSpec((1, tm, DIFF_WIDTH), row),
            pl.BlockSpec((1, 1, DIFF_QK, tm), tile),
            pl.BlockSpec((1, 1, DIFF_WIDTH, tm), tile),
            pl.BlockSpec((1, tm, GLA_WIDTH), row)],
        scratch_shapes=[
            pltpu.VMEM((tm + 8, GLA_CONV_WIDTH), F32),
            pltpu.VMEM((tm, GLA_WIDTH), F32),
            pltpu.VMEM((GLA_HEADS, GLA_DV, LANES), F32),
        ],
        compiler_params=pltpu.CompilerParams(
            dimension_semantics=("arbitrary", "arbitrary"), vmem_limit_bytes=VMEM_LIMIT),
        name="out_in_proj_gla" if fuse_prev else "in_proj_gla",
    )(x, *(prev or ()), mod_l, norm_g, w_nat, w_t, ones_bd, qg_full, kg, conv_w8, wgk_pad,
      bgk, tri, gla_g)


def _split_maps(qt_ref, s):
    qt = qt_ref[0, s]
    row = lax.broadcasted_iota(jnp.int32, qt.shape, 0)
    zero = jnp.zeros_like(qt)
    return (jnp.where(row < DIFF_DH, qt, zero), jnp.where(row >= DIFF_DH, qt, zero))


def _chunk_mask():
    kc = lax.broadcasted_iota(jnp.int32, (ATT_T, ATT_T), 0) // CHUNK
    qc = lax.broadcasted_iota(jnp.int32, (ATT_T, ATT_T), 1) // CHUNK
    return kc <= qc


def _values_with_ones(vt_ref, j):
    return jnp.concatenate([vt_ref[0, j], jnp.ones((ATT_LROWS, ATT_T), BF16)], axis=0)


def _attn_finalize(s, acc0, l0, acc1, l1, z_ref, lq1_ref, lk1_ref, lq2_ref, lk2_ref, li_ref,
                   g_ref, o_ref):
    lam_init = li_ref[:, 0:1]
    lam = (jnp.exp(jnp.sum(lq1_ref[...] * lk1_ref[...], axis=-1, keepdims=True))
           - jnp.exp(jnp.sum(lq2_ref[...] * lk2_ref[...], axis=-1, keepdims=True))
           + lam_init)
    out = acc0 / l0 - lam * (acc1 / l1)
    ms = jnp.mean(out * out, axis=0, keepdims=True)
    out = out * lax.rsqrt(ms + EPS) * g_ref[...] * (1.0 - lam_init)
    rows = pl.ds(pl.multiple_of(s * ATT_T, ATT_T), ATT_T)
    o_ref[0, rows, :] = (out.T * _silu(z_ref[0, rows, :].astype(F32))).astype(BF16)


def _attn_kernel_online(qt_ref, k_ref, vt_ref, z_ref, lq1_ref, lk1_ref, lq2_ref, lk2_ref,
                        li_ref, g_ref, o_ref, s_a, s_b, cm_a, cm_b, m_s, acc_s):
    T = ATT_T

    def tile(s, carry):
        i = pl.program_id(2) * ATT_QS + s
        qt_maps = _split_maps(qt_ref, s)

        m_s[...] = jnp.full(m_s.shape, -jnp.inf, F32)
        acc_s[...] = jnp.zeros(acc_s.shape, F32)

        def scores(j, s_ref, cm_ref):
            kj = k_ref[0, pl.ds(pl.multiple_of(j * T, T), T), :]
            for mp in range(2):
                s = jnp.dot(kj, qt_maps[mp], preferred_element_type=F32)
                s_ref[mp] = s
                cm_ref[mp] = jnp.max(s, axis=0, keepdims=True)

        def update(j, s_ref, cm_ref, mask):
            vtj = _values_with_ones(vt_ref, j)
            for mp in range(2):
                s = s_ref[mp]
                if mask is None:
                    tile_max = cm_ref[mp]
                else:
                    s = jnp.where(mask, s, -jnp.inf)
                    tile_max = jnp.max(s, axis=0, keepdims=True)
                m_old = m_s[mp]
                m_new = jnp.maximum(m_old, tile_max)
                alpha = jnp.exp2(m_old - m_new)
                p = jnp.exp2(s - m_new)
                acc_s[mp] = alpha * acc_s[mp] + jnp.dot(vtj, p.astype(BF16),
                                                        preferred_element_type=F32)
                m_s[mp] = m_new

        scores(0, s_a, cm_a)

        def pair(t, carry):
            j = 2 * t
            scores(j + 1, s_b, cm_b)
            update(j, s_a, cm_a, None)
            scores(j + 2, s_a, cm_a)
            update(j + 1, s_b, cm_b, None)
            return carry

        lax.fori_loop(0, i // 2, pair, 0)
        mask = _chunk_mask()

        @pl.when(i % 2 == 0)
        def _():
            update(i, s_a, cm_a, mask)

        @pl.when(i % 2 == 1)
        def _():
            scores(i, s_b, cm_b)
            update(i - 1, s_a, cm_a, None)
            update(i, s_b, cm_b, mask)

        acc0, acc1 = acc_s[0], acc_s[1]
        _attn_finalize(s, acc0[0:DIFF_DV], acc0[DIFF_DV:DIFF_DV + 1],
                       acc1[0:DIFF_DV], acc1[DIFF_DV:DIFF_DV + 1], z_ref,
                       lq1_ref, lk1_ref, lq2_ref, lk2_ref, li_ref, g_ref, o_ref)
        return carry

    lax.fori_loop(0, ATT_QS, tile, 0)


def _attn_kernel_bounded(qt_ref, k_ref, vt_ref, z_ref, lq1_ref, lk1_ref, lq2_ref, lk2_ref,
                         li_ref, g_ref, o_ref, acc_s, l_s):
    T, U = ATT_T, ATT_UNROLL

    def tile(s, carry):
        i = pl.program_id(2) * ATT_QS + s
        qt_maps = _split_maps(qt_ref, s)
        acc_s[...] = jnp.zeros(acc_s.shape, F32)
        l_s[...] = jnp.zeros(l_s.shape, F32)

        def group(j0, n, mask):
            ps = ([], [])
            lsum = [None, None]
            for u in range(n):
                kj = k_ref[0, pl.ds(pl.multiple_of((j0 + u) * T, T), T), :]
                for mp in range(2):
                    p = jnp.exp2(jnp.dot(kj, qt_maps[mp], preferred_element_type=F32))
                    if mask is not None and u == n - 1:
                        p = jnp.where(mask, p, 0.0)
                    psum = jnp.sum(p, axis=0, keepdims=True)
                    lsum[mp] = psum if lsum[mp] is None else lsum[mp] + psum
                    ps[mp].append(p.astype(BF16))
            vt = jnp.concatenate([vt_ref[0, j0 + u] for u in range(n)], axis=1)
            for mp in range(2):
                p_cat = jnp.concatenate(ps[mp], axis=0)
                acc_s[mp] += jnp.dot(vt, p_cat, preferred_element_type=F32)
                l_s[mp] += lsum[mp]

        def body(t, carry):
            group(t * U, U, None)
            return carry

        lax.fori_loop(0, i // U, body, 0)
        mask = _chunk_mask()
        rem = i % U
        for r in range(U):
            @pl.when(rem == r)
            def _():
                group(i - r, r + 1, mask)

        _attn_finalize(s, acc_s[0], l_s[0], acc_s[1], l_s[1],
                       z_ref, lq1_ref, lk1_ref, lq2_ref, lk2_ref, li_ref, g_ref, o_ref)
        return carry

    lax.fori_loop(0, ATT_QS, tile, 0)


def _diff_attention(bounded, qt, dk, vt, dz, lq1, lk1, lq2, lk2, lam_init_row, g_full):
    B, S, _ = dk.shape
    T = ATT_T
    H = DIFF_HEADS
    nt = S // T
    QS = ATT_QS
    qmap = lambda b, h, i: (b, i, h)
    const = lambda b, h, i: (0, 0)
    if bounded:
        body = _attn_kernel_bounded
        scratch = [pltpu.VMEM((2, DIFF_DV, T), F32), pltpu.VMEM((2, 1, T), F32)]
    else:
        body = _attn_kernel_online
        scratch = [pltpu.VMEM((2, T, T), F32), pltpu.VMEM((2, T, T), F32),
                   pltpu.VMEM((2, 1, T), F32), pltpu.VMEM((2, 1, T), F32),
                   pltpu.VMEM((2, 1, T), F32),
                   pltpu.VMEM((2, DIFF_DV + ATT_LROWS, T), F32)]
    return pl.pallas_call(
        body,
        out_shape=jax.ShapeDtypeStruct((B, S, DIFF_WIDTH), BF16),
        grid=(B, H, nt // QS),
        in_specs=[
            pl.BlockSpec((1, QS, 2 * DIFF_DH, T), lambda b, h, i: (b, i, h, 0)),
            pl.BlockSpec((1, S, LANES), lambda b, h, i: (b, 0, h)),
            pl.BlockSpec((1, nt, DIFF_DV, T), lambda b, h, i: (b, 0, h, 0)),
            pl.BlockSpec((1, QS * T, LANES), qmap),
            pl.BlockSpec((1, DIFF_DH), const),
            pl.BlockSpec((1, DIFF_DH), const),
            pl.BlockSpec((1, DIFF_DH), const),
            pl.BlockSpec((1, DIFF_DH), const),
            pl.BlockSpec((1, LANES), const),
            pl.BlockSpec((DIFF_DV, T), const),
        ],
        out_specs=pl.BlockSpec((1, QS * T, LANES), qmap),
        scratch_shapes=scratch,
        compiler_params=pltpu.CompilerParams(
            dimension_semantics=("arbitrary", "arbitrary", "arbitrary"),
            vmem_limit_bytes=VMEM_LIMIT),
        name="diff_attn_bounded" if bounded else "diff_attn_online",
    )(qt, dk, vt, dz, lq1, lk1, lq2, lk2, lam_init_row, g_full)


def _out_kernel(x_ref, og_ref, od_ref, mod_ref, w_ref, o_ref):
    o_ref[0] = _gated_residual(x_ref[0], og_ref, od_ref, mod_ref, w_ref)


def _out_projection(x, o_gla, o_diff, mod_l, w_out):
    B, S, D = x.shape
    tm = OUT_TM
    row = lambda b, i: (b, i, 0)
    return pl.pallas_call(
        _out_kernel,
        out_shape=jax.ShapeDtypeStruct((B, S, D), F32),
        grid=(B, S // tm),
        in_specs=[
            pl.BlockSpec((1, tm, D), row),
            pl.BlockSpec((1, tm, GLA_WIDTH), row),
            pl.BlockSpec((1, tm, DIFF_WIDTH), row),
            pl.BlockSpec((1, 1, 3 * D), lambda b, i: (b, 0, 0)),
            pl.BlockSpec((MIX_WIDTH, D), lambda b, i: (0, 0), pipeline_mode=pl.Buffered(1)),
        ],
        out_specs=pl.BlockSpec((1, tm, D), row),
        compiler_params=pltpu.CompilerParams(
            dimension_semantics=("arbitrary", "arbitrary"), vmem_limit_bytes=VMEM_LIMIT),
        name="out_proj",
    )(x, o_gla, o_diff, mod_l, w_out)


def _split_w_in(w):
    idx = [0]
    for s in SPLIT_SIZES:
        idx.append(idx[-1] + s)
    gq, gk, gv, glr, gz, dq, dk, dv, dz = [w[:, idx[n]:idx[n + 1]] for n in range(9)]
    glr = jnp.pad(glr, ((0, 0), (0, GATE_PAD - GLA_GATE_RANK)))
    w_nat = jnp.concatenate([gq, gk, gv, gz, dk, dz, glr], axis=1).astype(BF16)
    w_t = jnp.concatenate([dq, dv], axis=1).T.astype(BF16)
    return w_nat, w_t


def kernel(x, c, w_ada, b_ada, norm_g, w_in, conv_w, w_gk, b_gk, gla_norm_g,
           qn_g, kn_g, lam_q1, lam_k1, lam_q2, lam_k2, diff_norm_g, w_out):
    B, S, D = x.shape
    mod = _modulation(c, w_ada, b_ada)

    r = jnp.arange(DIFF_QK) // DIFF_DH
    ones_bd = (r[:, None] == r[None, :]).astype(BF16)
    t = jnp.arange(PROJ_TM)
    tri = ((t[:, None] // CHUNK == t[None, :] // CHUNK)
           & (t[None, :] <= t[:, None])).astype(BF16)

    prev = None
    for l in range(DEPTH):
        mod_l = mod[l].reshape(B, 1, 3 * D)
        w_nat, w_t = _split_w_in(w_in[l])
        qg_full = jnp.broadcast_to(
            jnp.tile(qn_g[l], DIFF_QK // DIFF_DH)[:, None], (DIFF_QK, PROJ_TM))
        kg = jnp.tile(kn_g[l], DIFF_QK // DIFF_DH).reshape(1, DIFF_QK)
        conv_w8 = jnp.pad(conv_w[l], ((0, 8 - CONV_K), (0, 0)))
        wgk_pad = jnp.pad(w_gk[l], ((0, GATE_PAD - GLA_GATE_RANK), (0, 0))).astype(BF16)
        outs = _in_projection(
            x, prev, mod_l, norm_g[l].reshape(1, D), w_nat, w_t, ones_bd, qg_full, kg,
            conv_w8, wgk_pad, b_gk[l].reshape(1, GLA_QK), tri, gla_norm_g[l].reshape(1, GLA_DV))
        if prev is not None:
            x, outs = outs[0], outs[1:]
        dk, dz, qt, vt, o_gla = outs

        lam_init = 0.8 - 0.6 * math.exp(-0.3 * l)
        score_bound = (DIFF_DH ** 0.5 * LOG2E * 1.01
                       * jnp.max(jnp.abs(qn_g[l])) * jnp.max(jnp.abs(kn_g[l])))
        attn_args = (
            qt, dk, vt, dz,
            lam_q1[l].reshape(1, DIFF_DH), lam_k1[l].reshape(1, DIFF_DH),
            lam_q2[l].reshape(1, DIFF_DH), lam_k2[l].reshape(1, DIFF_DH),
            jnp.full((1, LANES), lam_init, F32),
            jnp.broadcast_to(diff_norm_g[l][:, None], (DIFF_DV, ATT_T)))
        o_diff = lax.cond(score_bound <= ATT_SCORE_BOUND,
                          functools.partial(_diff_attention, True),
                          functools.partial(_diff_attention, False), *attn_args)

        prev = (o_gla, o_diff, mod_l, w_out[l].astype(BF16))
    return _out_projection(x, *prev)
```

```python
import functools
import math

import jax
import jax.numpy as jnp
from jax import lax
from jax.experimental import pallas as pl
from jax.experimental.pallas import tpu as pltpu

D_MODEL = 1024
DEPTH = 4
CHUNK = 64
CONV_K = 4
EPS = 1e-6

GLA_HEADS = 4
GLA_DK = 64
GLA_DV = 128
GLA_QK = GLA_HEADS * GLA_DK
GLA_WIDTH = GLA_HEADS * GLA_DV
GLA_GATE_RANK = 16
GLA_GATE_TEMP = 16.0

DIFF_HEADS = 4
DIFF_DH = 64
DIFF_DV = 2 * DIFF_DH
DIFF_QK = DIFF_HEADS * 2 * DIFF_DH
DIFF_WIDTH = DIFF_HEADS * DIFF_DV

MIX_WIDTH = GLA_WIDTH + DIFF_WIDTH
GLA_CONV_WIDTH = 2 * GLA_QK + GLA_WIDTH
SPLIT_SIZES = (GLA_QK, GLA_QK, GLA_WIDTH, GLA_GATE_RANK, GLA_WIDTH,
               DIFF_QK, DIFF_QK, DIFF_WIDTH, DIFF_WIDTH)

LANES = 128
GATE_PAD = LANES
VMEM_LIMIT = 56 * 1024 * 1024

ATT_T = 512
ATT_QS = 4
ATT_UNROLL = 8
ATT_SCORE_BOUND = 32.0
ATT_LROWS = 16
PROJ_TM = ATT_T
OUT_TM = 512
MOD_TN = 512

F32 = jnp.float32
BF16 = jnp.bfloat16
LOG2E = math.log2(math.e)


def _sigmoid(x):
    return 1.0 / (1.0 + jnp.exp(-x))


def _silu(x):
    return x * _sigmoid(x)


def _mod_kernel(ct_ref, w_ref, b_ref, o_ref):
    ct = ct_ref[...]
    ca = _silu(ct)
    w = w_ref[0]
    rows = []
    for b in range(ct.shape[1]):
        rows.append(jnp.sum(w * ca[:, b:b + 1], axis=0, keepdims=True))
    o_ref[0] = jnp.concatenate(rows, axis=0) + b_ref[0]


def _modulation(c, w_ada, b_ada):
    B, D = c.shape
    L, _, N = w_ada.shape
    return pl.pallas_call(
        _mod_kernel,
        out_shape=jax.ShapeDtypeStruct((L, B, N), F32),
        grid=(L, N // MOD_TN),
        in_specs=[
            pl.BlockSpec((D, B), lambda l, n: (0, 0)),
            pl.BlockSpec((1, D, MOD_TN), lambda l, n: (l, 0, n)),
            pl.BlockSpec((1, 1, MOD_TN), lambda l, n: (l, 0, n)),
        ],
        out_specs=pl.BlockSpec((1, B, MOD_TN), lambda l, n: (l, 0, n)),
        compiler_params=pltpu.CompilerParams(
            dimension_semantics=("arbitrary", "arbitrary"), vmem_limit_bytes=VMEM_LIMIT),
        name="adaln_mod",
    )(c.T, w_ada, b_ada.reshape(L, 1, N))


def _gated_residual(x, og_ref, od_ref, mod_ref, w_ref):
    gate = mod_ref[0][:, 2 * D_MODEL:3 * D_MODEL]
    y = (jnp.dot(og_ref[0], w_ref[0:GLA_WIDTH, :], preferred_element_type=F32)
         + jnp.dot(od_ref[0], w_ref[GLA_WIDTH:MIX_WIDTH, :], preferred_element_type=F32))
    return x + gate * y


def _group_rms(t, ones_bd, gain):
    ss = jnp.dot((t * t).astype(BF16), ones_bd, preferred_element_type=F32)
    return t * lax.rsqrt(ss * (1.0 / DIFF_DH) + EPS) * gain


def _gla_log_decay(gate, tri_ref):
    log_a = -(jnp.maximum(-gate, 0.0) + jnp.log1p(jnp.exp(-jnp.abs(gate))))
    log_a = log_a * (1.0 / GLA_GATE_TEMP)
    la_hi = log_a.astype(BF16)
    la_lo = (log_a - la_hi.astype(F32)).astype(BF16)
    tri = tri_ref[...]
    return (jnp.dot(tri, la_hi, preferred_element_type=F32)
            + jnp.dot(tri, la_lo, preferred_element_type=F32))


def _gla_conv(xbuf, cw_ref):
    T = PROJ_TM
    cw = cw_ref[...]
    conv = cw[0:1, :] * xbuf[pl.ds(8 - (CONV_K - 1), T), :]
    for j in range(1, CONV_K):
        conv = conv + cw[j:j + 1, :] * xbuf[pl.ds(8 - (CONV_K - 1) + j, T), :]
    xbuf[0:8, :] = xbuf[T:T + 8, :]
    act = _silu(conv)
    q = (act[:, 0:GLA_QK] * (GLA_DK ** -0.5)).astype(BF16)
    k = act[:, GLA_QK:2 * GLA_QK]
    v = act[:, 2 * GLA_QK:].astype(BF16)
    return q, k, v


def _gla_tile(q, k, v, zbuf, state, bcum, g_ref, o_ref):
    T = PROJ_TM
    n_chunks = T // CHUNK

    lane = lax.broadcasted_iota(jnp.int32, (CHUNK, LANES), 1)
    head_lanes = [(lane >= hh * GLA_DK) & (lane < (hh + 1) * GLA_DK) for hh in range(2)]

    decay, outer = [], []
    for c in range(n_chunks):
        r0 = c * CHUNK
        b_c = bcum[r0:r0 + CHUNK, :]
        b_end = b_c[CHUNK - 1:CHUNK, :]
        kdec = k[r0:r0 + CHUNK, :] * jnp.exp(b_end - b_c)
        decay.append(jnp.exp(b_end))
        for h in range(GLA_HEADS):
            p = h // 2
            kd_m = jnp.where(head_lanes[h % 2], kdec[:, p * LANES:(p + 1) * LANES], 0.0)
            v_h = v[r0:r0 + CHUNK, h * GLA_DV:(h + 1) * GLA_DV]
            outer.append(lax.dot_general(v_h, kd_m.astype(BF16), (((0,), (0,)), ((), ())),
                                         preferred_element_type=F32))

    g_out = g_ref[...]
    for c in range(n_chunks):
        r0 = c * CHUNK
        for h in range(GLA_HEADS):
            p = h // 2
            s_new = state[h] * decay[c][:, p * LANES:(p + 1) * LANES] + outer[c * GLA_HEADS + h]
            state[h] = s_new
            q_pair = q[r0:r0 + CHUNK, p * LANES:(p + 1) * LANES]
            o_h = lax.dot_general(q_pair, s_new.astype(BF16), (((1,), (1,)), ((), ())),
                                  preferred_element_type=F32)
            ms = jnp.mean(o_h * o_h, axis=-1, keepdims=True)
            zz = zbuf[r0:r0 + CHUNK, h * GLA_DV:(h + 1) * GLA_DV]
            res = o_h * lax.rsqrt(ms + EPS) * g_out * _silu(zz)
            o_ref[0, r0:r0 + CHUNK, h * GLA_DV:(h + 1) * GLA_DV] = res.astype(BF16)


N_PROJ_INPUTS = 13
N_PREV_INPUTS = 4


def _proj_kernel(*refs, fuse_prev):
    if fuse_prev:
        ogp_ref, odp_ref, modp_ref, wout_ref = refs[1:1 + N_PREV_INPUTS]
        xo_ref = refs[N_PROJ_INPUTS + N_PREV_INPUTS]
        refs = (refs[:1] + refs[1 + N_PREV_INPUTS:N_PROJ_INPUTS + N_PREV_INPUTS]
                + refs[N_PROJ_INPUTS + N_PREV_INPUTS + 1:])
    (x_ref, mod_ref, g_ref, w_ref, wt_ref, ones_ref, qg_ref, kg_ref,
     cw_ref, wgk_ref, bgk_ref, tri_ref, gg_ref,
     dk_ref, dz_ref, qt_ref, vt_ref, og_ref, xbuf, zbuf, state) = refs
    D = D_MODEL
    T = PROJ_TM

    @pl.when(pl.program_id(1) == 0)
    def _():
        xbuf[0:8, :] = jnp.zeros((8, GLA_CONV_WIDTH), F32)
        state[...] = jnp.zeros(state.shape, F32)

    x = x_ref[0]
    if fuse_prev:
        x = _gated_residual(x, ogp_ref, odp_ref, modp_ref, wout_ref)
        xo_ref[0] = x
    mod = mod_ref[0]
    shift = mod[:, 0:D]
    scale = mod[:, D:2 * D]
    ms = jnp.mean(x * x, axis=-1, keepdims=True)
    y = x * lax.rsqrt(ms + EPS) * g_ref[...]
    h = (y * (1.0 + scale) + shift).astype(BF16)

    def proj(a, n):
        return jnp.dot(h, w_ref[:, a:a + n], preferred_element_type=F32)

    def proj_t(a, n):
        return lax.dot_general(wt_ref[a:a + n, :], h, (((1,), (1,)), ((), ())),
                               preferred_element_type=F32)

    xbuf[8:8 + T, 0:512] = proj(0, 512)
    xbuf[8:8 + T, 512:1024] = proj(512, 512)
    gla_q, gla_k, gla_v = _gla_conv(xbuf, cw_ref)
    glr = proj(2560, GATE_PAD).astype(BF16)
    zbuf[...] = proj(1024, 512)
    dk_raw = proj(1536, 512)
    gate = jnp.dot(glr, wgk_ref[...], preferred_element_type=F32) + bgk_ref[...]
    dz_ref[0] = proj(2048, 512).astype(BF16)
    dk_ref[0] = _group_rms(dk_raw, ones_ref[...], kg_ref[...]).astype(BF16)

    qt = proj_t(0, DIFF_QK)
    bcum = _gla_log_decay(gate, tri_ref)
    vt_ref[0, 0] = proj_t(DIFF_QK, DIFF_WIDTH).astype(BF16)
    for g in range(DIFF_QK // DIFF_DH):
        r0 = g * DIFF_DH
        t = qt[r0:r0 + DIFF_DH, :]
        ss = jnp.sum(t * t, axis=0, keepdims=True)
        t = t * lax.rsqrt(ss * (1.0 / DIFF_DH) + EPS) * qg_ref[r0:r0 + DIFF_DH, :]
        qt_ref[0, 0, r0:r0 + DIFF_DH, :] = (t * (DIFF_DH ** -0.5 * LOG2E)).astype(BF16)

    _gla_tile(gla_q, gla_k, gla_v, zbuf, state, bcum, gg_ref, og_ref)


def _in_projection(x, prev, mod_l, norm_g, w_nat, w_t, ones_bd, qg_full, kg,
                   conv_w8, wgk_pad, bgk, tri, gla_g):
    B, S, D = x.shape
    tm = PROJ_TM
    NW = w_nat.shape[1]
    row = lambda b, i: (b, i, 0)
    tile = lambda b, i: (b, i, 0, 0)
    const = lambda b, i: (0, 0)
    per_batch = lambda b, i: (b, 0, 0)
    once = pl.Buffered(1)
    fuse_prev = prev is not None
    prev_specs = [pl.BlockSpec((1, tm, GLA_WIDTH), row),
                  pl.BlockSpec((1, tm, DIFF_WIDTH), row),
                  pl.BlockSpec((1, 1, 3 * D), per_batch),
                  pl.BlockSpec((MIX_WIDTH, D), const, pipeline_mode=once)] if fuse_prev else []
    x_out_shape = [jax.ShapeDtypeStruct((B, S, D), F32)] if fuse_prev else []
    x_out_spec = [pl.BlockSpec((1, tm, D), row)] if fuse_prev else []
    return pl.pallas_call(
        functools.partial(_proj_kernel, fuse_prev=fuse_prev),
        out_shape=x_out_shape + [
            jax.ShapeDtypeStruct((B, S, DIFF_QK), BF16),
            jax.ShapeDtypeStruct((B, S, DIFF_WIDTH), BF16),
            jax.ShapeDtypeStruct((B, S // tm, DIFF_QK, tm), BF16),
            jax.ShapeDtypeStruct((B, S // tm, DIFF_WIDTH, tm), BF16),
            jax.ShapeDtypeStruct((B, S, GLA_WIDTH), BF16)],
        grid=(B, S // tm),
        in_specs=[pl.BlockSpec((1, tm, D), row)] + prev_specs + [
            pl.BlockSpec((1, 1, 3 * D), per_batch),
            pl.BlockSpec((1, D), const),
            pl.BlockSpec((D, NW), const, pipeline_mode=once),
            pl.BlockSpec((DIFF_QK + DIFF_WIDTH, D), const, pipeline_mode=once),
            pl.BlockSpec((512, 512), const, pipeline_mode=once),
            pl.BlockSpec((DIFF_QK, tm), const, pipeline_mode=once),
            pl.BlockSpec((1, 512), const),
            pl.BlockSpec((8, GLA_CONV_WIDTH), const),
            pl.BlockSpec((GATE_PAD, GLA_QK), const),
            pl.BlockSpec((1, GLA_QK), const),
            pl.BlockSpec((tm, tm), const, pipeline_mode=once),
            pl.BlockSpec((1, GLA_DV), const),
        ],
        out_specs=x_out_spec + [
            pl.BlockSpec((1, tm, DIFF_QK), row),
            pl.BlockSpec((1, tm, DIFF_WIDTH), row),
            pl.BlockSpec((1, 1, DIFF_QK, tm), tile),
            pl.BlockSpec((1, 1, DIFF_WIDTH, tm), tile),
            pl.BlockSpec((1, tm, GLA_WIDTH), row)],
        scratch_shapes=[
            pltpu.VMEM((tm + 8, GLA_CONV_WIDTH), F32),
            pltpu.VMEM((tm, GLA_WIDTH), F32),
            pltpu.VMEM((GLA_HEADS, GLA_DV, LANES), F32),
        ],
        compiler_params=pltpu.CompilerParams(
            dimension_semantics=("arbitrary", "arbitrary"), vmem_limit_bytes=VMEM_LIMIT),
        name="out_in_proj_gla" if fuse_prev else "in_proj_gla",
    )(x, *(prev or ()), mod_l, norm_g, w_nat, w_t, ones_bd, qg_full, kg, conv_w8, wgk_pad,
      bgk, tri, gla_g)


def _split_maps(qt_ref, s):
    qt = qt_ref[0, s]
    row = lax.broadcasted_iota(jnp.int32, qt.shape, 0)
    zero = jnp.zeros_like(qt)
    return (jnp.where(row < DIFF_DH, qt, zero), jnp.where(row >= DIFF_DH, qt, zero))


def _chunk_mask():
    kc = lax.broadcasted_iota(jnp.int32, (ATT_T, ATT_T), 0) // CHUNK
    qc = lax.broadcasted_iota(jnp.int32, (ATT_T, ATT_T), 1) // CHUNK
    return kc <= qc


def _values_with_ones(vt_ref, j):
    return jnp.concatenate([vt_ref[0, j], jnp.ones((ATT_LROWS, ATT_T), BF16)], axis=0)


def _attn_finalize(s, acc0, l0, acc1, l1, z_ref, lq1_ref, lk1_ref, lq2_ref, lk2_ref, li_ref,
                   g_ref, o_ref):
    lam_init = li_ref[:, 0:1]
    lam = (jnp.exp(jnp.sum(lq1_ref[...] * lk1_ref[...], axis=-1, keepdims=True))
           - jnp.exp(jnp.sum(lq2_ref[...] * lk2_ref[...], axis=-1, keepdims=True))
           + lam_init)
    out = acc0 / l0 - lam * (acc1 / l1)
    ms = jnp.mean(out * out, axis=0, keepdims=True)
    out = out * lax.rsqrt(ms + EPS) * g_ref[...] * (1.0 - lam_init)
    start = s * ATT_T
    rows = pl.ds(start if isinstance(s, int) else pl.multiple_of(start, ATT_T), ATT_T)
    o_ref[0, rows, :] = (out.T * _silu(z_ref[0, rows, :].astype(F32))).astype(BF16)


def _attn_kernel_online(qt_ref, k_ref, vt_ref, z_ref, lq1_ref, lk1_ref, lq2_ref, lk2_ref,
                        li_ref, g_ref, o_ref, s_a, s_b, cm_a, cm_b, m_s, acc_s):
    T = ATT_T

    def tile(s, carry):
        i = pl.program_id(2) * ATT_QS + s
        qt_maps = _split_maps(qt_ref, s)

        m_s[...] = jnp.full(m_s.shape, -jnp.inf, F32)
        acc_s[...] = jnp.zeros(acc_s.shape, F32)

        def scores(j, s_ref, cm_ref):
            kj = k_ref[0, pl.ds(pl.multiple_of(j * T, T), T), :]
            for mp in range(2):
                s = jnp.dot(kj, qt_maps[mp], preferred_element_type=F32)
                s_ref[mp] = s
                cm_ref[mp] = jnp.max(s, axis=0, keepdims=True)

        def update(j, s_ref, cm_ref, mask):
            vtj = _values_with_ones(vt_ref, j)
            for mp in range(2):
                s = s_ref[mp]
                if mask is None:
                    tile_max = cm_ref[mp]
                else:
                    s = jnp.where(mask, s, -jnp.inf)
                    tile_max = jnp.max(s, axis=0, keepdims=True)
                m_old = m_s[mp]
                m_new = jnp.maximum(m_old, tile_max)
                alpha = jnp.exp2(m_old - m_new)
                p = jnp.exp2(s - m_new)
                acc_s[mp] = alpha * acc_s[mp] + jnp.dot(vtj, p.astype(BF16),
                                                        preferred_element_type=F32)
                m_s[mp] = m_new

        scores(0, s_a, cm_a)

        def pair(t, carry):
            j = 2 * t
            scores(j + 1, s_b, cm_b)
            update(j, s_a, cm_a, None)
            scores(j + 2, s_a, cm_a)
            update(j + 1, s_b, cm_b, None)
            return carry

        lax.fori_loop(0, i // 2, pair, 0)
        mask = _chunk_mask()

        @pl.when(i % 2 == 0)
        def _():
            update(i, s_a, cm_a, mask)

        @pl.when(i % 2 == 1)
        def _():
            scores(i, s_b, cm_b)
            update(i - 1, s_a, cm_a, None)
            update(i, s_b, cm_b, mask)

        acc0, acc1 = acc_s[0], acc_s[1]
        _attn_finalize(s, acc0[0:DIFF_DV], acc0[DIFF_DV:DIFF_DV + 1],
                       acc1[0:DIFF_DV], acc1[DIFF_DV:DIFF_DV + 1], z_ref,
                       lq1_ref, lk1_ref, lq2_ref, lk2_ref, li_ref, g_ref, o_ref)
        return carry

    lax.fori_loop(0, ATT_QS, tile, 0)


def _attn_kernel_bounded(qt_ref, k_ref, vt_ref, z_ref, lq1_ref, lk1_ref, lq2_ref, lk2_ref,
                         li_ref, g_ref, o_ref, acc_s, l_s):
    T, U = ATT_T, ATT_UNROLL

    def finalize(slot, s_out):
        _attn_finalize(s_out, acc_s[slot, 0], l_s[slot, 0], acc_s[slot, 1], l_s[slot, 1],
                       z_ref, lq1_ref, lk1_ref, lq2_ref, lk2_ref, li_ref, g_ref, o_ref)

    acc_s[1] = jnp.zeros(acc_s.shape[1:], F32)
    l_s[1] = jnp.ones(l_s.shape[1:], F32)

    def tile(s, carry):
        i = pl.program_id(2) * ATT_QS + s
        slot = s % 2
        qt_maps = _split_maps(qt_ref, s)
        acc_s[slot] = jnp.zeros(acc_s.shape[1:], F32)
        l_s[slot] = jnp.zeros(l_s.shape[1:], F32)

        def group(j0, n, mask):
            ps = ([], [])
            lsum = [None, None]
            for u in range(n):
                kj = k_ref[0, pl.ds(pl.multiple_of((j0 + u) * T, T), T), :]
                for mp in range(2):
                    p = jnp.exp2(jnp.dot(kj, qt_maps[mp], preferred_element_type=F32))
                    if mask is not None and u == n - 1:
                        p = jnp.where(mask, p, 0.0)
                    psum = jnp.sum(p, axis=0, keepdims=True)
                    lsum[mp] = psum if lsum[mp] is None else lsum[mp] + psum
                    ps[mp].append(p.astype(BF16))
            vt = jnp.concatenate([vt_ref[0, j0 + u] for u in range(n)], axis=1)
            for mp in range(2):
                p_cat = jnp.concatenate(ps[mp], axis=0)
                acc_s[slot, mp] += jnp.dot(vt, p_cat, preferred_element_type=F32)
                l_s[slot, mp] += lsum[mp]

        mask = _chunk_mask()
        rem = i % U
        for r in range(U):
            @pl.when(rem == r)
            def _():
                finalize(1 - slot, jnp.maximum(s - 1, 0))
                group(i - r, r + 1, mask)

        def body(t, carry):
            group(t * U, U, None)
            return carry

        lax.fori_loop(0, i // U, body, 0)
        return carry

    lax.fori_loop(0, ATT_QS, tile, 0)
    finalize((ATT_QS - 1) % 2, ATT_QS - 1)


def _diff_attention(bounded, qt, dk, vt, dz, lq1, lk1, lq2, lk2, lam_init_row, g_full):
    B, S, _ = dk.shape
    T = ATT_T
    H = DIFF_HEADS
    nt = S // T
    QS = ATT_QS
    qmap = lambda b, h, i: (b, i, h)
    const = lambda b, h, i: (0, 0)
    if bounded:
        body = _attn_kernel_bounded
        scratch = [pltpu.VMEM((2, 2, DIFF_DV, T), F32), pltpu.VMEM((2, 2, 1, T), F32)]
    else:
        body = _attn_kernel_online
        scratch = [pltpu.VMEM((2, T, T), F32), pltpu.VMEM((2, T, T), F32),
                   pltpu.VMEM((2, 1, T), F32), pltpu.VMEM((2, 1, T), F32),
                   pltpu.VMEM((2, 1, T), F32),
                   pltpu.VMEM((2, DIFF_DV + ATT_LROWS, T), F32)]
    return pl.pallas_call(
        body,
        out_shape=jax.ShapeDtypeStruct((B, S, DIFF_WIDTH), BF16),
        grid=(B, H, nt // QS),
        in_specs=[
            pl.BlockSpec((1, QS, 2 * DIFF_DH, T), lambda b, h, i: (b, i, h, 0)),
            pl.BlockSpec((1, S, LANES), lambda b, h, i: (b, 0, h)),
            pl.BlockSpec((1, nt, DIFF_DV, T), lambda b, h, i: (b, 0, h, 0)),
            pl.BlockSpec((1, QS * T, LANES), qmap),
            pl.BlockSpec((1, DIFF_DH), const),
            pl.BlockSpec((1, DIFF_DH), const),
            pl.BlockSpec((1, DIFF_DH), const),
            pl.BlockSpec((1, DIFF_DH), const),
            pl.BlockSpec((1, LANES), const),
            pl.BlockSpec((DIFF_DV, T), const),
        ],
        out_specs=pl.BlockSpec((1, QS * T, LANES), qmap),
        scratch_shapes=scratch,
        compiler_params=pltpu.CompilerParams(
            dimension_semantics=("arbitrary", "arbitrary", "arbitrary"),
            vmem_limit_bytes=VMEM_LIMIT),
        name="diff_attn_bounded" if bounded else "diff_attn_online",
    )(qt, dk, vt, dz, lq1, lk1, lq2, lk2, lam_init_row, g_full)


def _out_kernel(x_ref, og_ref, od_ref, mod_ref, w_ref, o_ref):
    o_ref[0] = _gated_residual(x_ref[0], og_ref, od_ref, mod_ref, w_ref)


def _out_projection(x, o_gla, o_diff, mod_l, w_out):
    B, S, D = x.shape
    tm = OUT_TM
    row = lambda b, i: (b, i, 0)
    return pl.pallas_call(
        _out_kernel,
        out_shape=jax.ShapeDtypeStruct((B, S, D), F32),
        grid=(B, S // tm),
        in_specs=[
            pl.BlockSpec((1, tm, D), row),
            pl.BlockSpec((1, tm, GLA_WIDTH), row),
            pl.BlockSpec((1, tm, DIFF_WIDTH), row),
            pl.BlockSpec((1, 1, 3 * D), lambda b, i: (b, 0, 0)),
            pl.BlockSpec((MIX_WIDTH, D), lambda b, i: (0, 0), pipeline_mode=pl.Buffered(1)),
        ],
        out_specs=pl.BlockSpec((1, tm, D), row),
        compiler_params=pltpu.CompilerParams(
            dimension_semantics=("arbitrary", "arbitrary"), vmem_limit_bytes=VMEM_LIMIT),
        name="out_proj",
    )(x, o_gla, o_diff, mod_l, w_out)


def _split_w_in(w):
    idx = [0]
    for s in SPLIT_SIZES:
        idx.append(idx[-1] + s)
    gq, gk, gv, glr, gz, dq, dk, dv, dz = [w[:, idx[n]:idx[n + 1]] for n in range(9)]
    glr = jnp.pad(glr, ((0, 0), (0, GATE_PAD - GLA_GATE_RANK)))
    w_nat = jnp.concatenate([gq, gk, gv, gz, dk, dz, glr], axis=1).astype(BF16)
    w_t = jnp.concatenate([dq, dv], axis=1).T.astype(BF16)
    return w_nat, w_t


def kernel(x, c, w_ada, b_ada, norm_g, w_in, conv_w, w_gk, b_gk, gla_norm_g,
           qn_g, kn_g, lam_q1, lam_k1, lam_q2, lam_k2, diff_norm_g, w_out):
    B, S, D = x.shape
    mod = _modulation(c, w_ada, b_ada)

    r = jnp.arange(DIFF_QK) // DIFF_DH
    ones_bd = (r[:, None] == r[None, :]).astype(BF16)
    t = jnp.arange(PROJ_TM)
    tri = ((t[:, None] // CHUNK == t[None, :] // CHUNK)
           & (t[None, :] <= t[:, None])).astype(BF16)

    prev = None
    for l in range(DEPTH):
        mod_l = mod[l].reshape(B, 1, 3 * D)
        w_nat, w_t = _split_w_in(w_in[l])
        qg_full = jnp.broadcast_to(
            jnp.tile(qn_g[l], DIFF_QK // DIFF_DH)[:, None], (DIFF_QK, PROJ_TM))
        kg = jnp.tile(kn_g[l], DIFF_QK // DIFF_DH).reshape(1, DIFF_QK)
        conv_w8 = jnp.pad(conv_w[l], ((0, 8 - CONV_K), (0, 0)))
        wgk_pad = jnp.pad(w_gk[l], ((0, GATE_PAD - GLA_GATE_RANK), (0, 0))).astype(BF16)
        outs = _in_projection(
            x, prev, mod_l, norm_g[l].reshape(1, D), w_nat, w_t, ones_bd, qg_full, kg,
            conv_w8, wgk_pad, b_gk[l].reshape(1, GLA_QK), tri, gla_norm_g[l].reshape(1, GLA_DV))
        if prev is not None:
            x, outs = outs[0], outs[1:]
        dk, dz, qt, vt, o_gla = outs

        lam_init = 0.8 - 0.6 * math.exp(-0.3 * l)
        score_bound = (DIFF_DH ** 0.5 * LOG2E * 1.01
                       * jnp.max(jnp.abs(qn_g[l])) * jnp.max(jnp.abs(kn_g[l])))
        attn_args = (
            qt, dk, vt, dz,
            lam_q1[l].reshape(1, DIFF_DH), lam_k1[l].reshape(1, DIFF_DH),
            lam_q2[l].reshape(1, DIFF_DH), lam_k2[l].reshape(1, DIFF_DH),
            jnp.full((1, LANES), lam_init, F32),
            jnp.broadcast_to(diff_norm_g[l][:, None], (DIFF_DV, ATT_T)))
        o_diff = lax.cond(score_bound <= ATT_SCORE_BOUND,
                          functools.partial(_diff_attention, True),
                          functools.partial(_diff_attention, False), *attn_args)

        prev = (o_gla, o_diff, mod_l, w_out[l].astype(BF16))
    return _out_projection(x, *prev)
```

```python
import functools
import math

import jax
import jax.numpy as jnp
from jax import lax
from jax.experimental import pallas as pl
from jax.experimental.pallas import tpu as pltpu

D_MODEL = 1024
DEPTH = 4
CHUNK = 64
CONV_K = 4
EPS = 1e-6

GLA_HEADS = 4
GLA_DK = 64
GLA_DV = 128
GLA_QK = GLA_HEADS * GLA_DK
GLA_WIDTH = GLA_HEADS * GLA_DV
GLA_GATE_RANK = 16
GLA_GATE_TEMP = 16.0

DIFF_HEADS = 4
DIFF_DH = 64
DIFF_DV = 2 * DIFF_DH
DIFF_QK = DIFF_HEADS * 2 * DIFF_DH
DIFF_WIDTH = DIFF_HEADS * DIFF_DV

MIX_WIDTH = GLA_WIDTH + DIFF_WIDTH
GLA_CONV_WIDTH = 2 * GLA_QK + GLA_WIDTH
SPLIT_SIZES = (GLA_QK, GLA_QK, GLA_WIDTH, GLA_GATE_RANK, GLA_WIDTH,
               DIFF_QK, DIFF_QK, DIFF_WIDTH, DIFF_WIDTH)

LANES = 128
GATE_PAD = LANES
VMEM_LIMIT = 56 * 1024 * 1024

ATT_T = 512
ATT_QS = 8
ATT_UNROLL = 8
ATT_SCORE_BOUND = 32.0
ATT_LROWS = 16
PROJ_TM = ATT_T
OUT_TM = 512
MOD_TN = 512

F32 = jnp.float32
BF16 = jnp.bfloat16
LOG2E = math.log2(math.e)


def _sigmoid(x):
    return 1.0 / (1.0 + jnp.exp(-x))


def _silu(x):
    return x * _sigmoid(x)


def _mod_kernel(ct_ref, w_ref, b_ref, o_ref):
    ct = ct_ref[...]
    ca = _silu(ct)
    w = w_ref[0]
    rows = []
    for b in range(ct.shape[1]):
        rows.append(jnp.sum(w * ca[:, b:b + 1], axis=0, keepdims=True))
    o_ref[0] = jnp.concatenate(rows, axis=0) + b_ref[0]


def _modulation(c, w_ada, b_ada):
    B, D = c.shape
    L, _, N = w_ada.shape
    return pl.pallas_call(
        _mod_kernel,
        out_shape=jax.ShapeDtypeStruct((L, B, N), F32),
        grid=(L, N // MOD_TN),
        in_specs=[
            pl.BlockSpec((D, B), lambda l, n: (0, 0)),
            pl.BlockSpec((1, D, MOD_TN), lambda l, n: (l, 0, n)),
            pl.BlockSpec((1, 1, MOD_TN), lambda l, n: (l, 0, n)),
        ],
        out_specs=pl.BlockSpec((1, B, MOD_TN), lambda l, n: (l, 0, n)),
        compiler_params=pltpu.CompilerParams(
            dimension_semantics=("arbitrary", "arbitrary"), vmem_limit_bytes=VMEM_LIMIT),
        name="adaln_mod",
    )(c.T, w_ada, b_ada.reshape(L, 1, N))


def _gated_residual(x, og_ref, od_ref, mod_ref, w_ref):
    gate = mod_ref[0][:, 2 * D_MODEL:3 * D_MODEL]
    y = (jnp.dot(og_ref[0], w_ref[0:GLA_WIDTH, :], preferred_element_type=F32)
         + jnp.dot(od_ref[0], w_ref[GLA_WIDTH:MIX_WIDTH, :], preferred_element_type=F32))
    return x + gate * y


def _group_rms(t, ones_bd, gain):
    ss = jnp.dot((t * t).astype(BF16), ones_bd, preferred_element_type=F32)
    return t * lax.rsqrt(ss * (1.0 / DIFF_DH) + EPS) * gain


def _gla_log_decay(gate, tri_ref):
    log_a = -(jnp.maximum(-gate, 0.0) + jnp.log1p(jnp.exp(-jnp.abs(gate))))
    log_a = log_a * (1.0 / GLA_GATE_TEMP)
    la_hi = log_a.astype(BF16)
    la_lo = (log_a - la_hi.astype(F32)).astype(BF16)
    tri = tri_ref[...]
    return (jnp.dot(tri, la_hi, preferred_element_type=F32)
            + jnp.dot(tri, la_lo, preferred_element_type=F32))


def _gla_conv(xbuf, cw_ref):
    T = PROJ_TM
    cw = cw_ref[...]
    conv = cw[0:1, :] * xbuf[pl.ds(8 - (CONV_K - 1), T), :]
    for j in range(1, CONV_K):
        conv = conv + cw[j:j + 1, :] * xbuf[pl.ds(8 - (CONV_K - 1) + j, T), :]
    xbuf[0:8, :] = xbuf[T:T + 8, :]
    act = _silu(conv)
    q = (act[:, 0:GLA_QK] * (GLA_DK ** -0.5)).astype(BF16)
    k = act[:, GLA_QK:2 * GLA_QK]
    v = act[:, 2 * GLA_QK:].astype(BF16)
    return q, k, v


def _gla_tile(q, k, v, zbuf, state, bcum, g_ref, o_ref):
    T = PROJ_TM
    n_chunks = T // CHUNK

    lane = lax.broadcasted_iota(jnp.int32, (CHUNK, LANES), 1)
    head_lanes = [(lane >= hh * GLA_DK) & (lane < (hh + 1) * GLA_DK) for hh in range(2)]

    decay, outer = [], []
    for c in range(n_chunks):
        r0 = c * CHUNK
        b_c = bcum[r0:r0 + CHUNK, :]
        b_end = b_c[CHUNK - 1:CHUNK, :]
        kdec = k[r0:r0 + CHUNK, :] * jnp.exp(b_end - b_c)
        decay.append(jnp.exp(b_end))
        for h in range(GLA_HEADS):
            p = h // 2
            kd_m = jnp.where(head_lanes[h % 2], kdec[:, p * LANES:(p + 1) * LANES], 0.0)
            v_h = v[r0:r0 + CHUNK, h * GLA_DV:(h + 1) * GLA_DV]
            outer.append(lax.dot_general(v_h, kd_m.astype(BF16), (((0,), (0,)), ((), ())),
                                         preferred_element_type=F32))

    g_out = g_ref[...]
    for c in range(n_chunks):
        r0 = c * CHUNK
        for h in range(GLA_HEADS):
            p = h // 2
            s_new = state[h] * decay[c][:, p * LANES:(p + 1) * LANES] + outer[c * GLA_HEADS + h]
            state[h] = s_new
            q_pair = q[r0:r0 + CHUNK, p * LANES:(p + 1) * LANES]
            o_h = lax.dot_general(q_pair, s_new.astype(BF16), (((1,), (1,)), ((), ())),
                                  preferred_element_type=F32)
            ms = jnp.mean(o_h * o_h, axis=-1, keepdims=True)
            zz = zbuf[r0:r0 + CHUNK, h * GLA_DV:(h + 1) * GLA_DV]
            res = o_h * lax.rsqrt(ms + EPS) * g_out * _silu(zz)
            o_ref[0, r0:r0 + CHUNK, h * GLA_DV:(h + 1) * GLA_DV] = res.astype(BF16)


N_PROJ_INPUTS = 13
N_PREV_INPUTS = 4


def _proj_kernel(*refs, fuse_prev):
    if fuse_prev:
        ogp_ref, odp_ref, modp_ref, wout_ref = refs[1:1 + N_PREV_INPUTS]
        xo_ref = refs[N_PROJ_INPUTS + N_PREV_INPUTS]
        refs = (refs[:1] + refs[1 + N_PREV_INPUTS:N_PROJ_INPUTS + N_PREV_INPUTS]
                + refs[N_PROJ_INPUTS + N_PREV_INPUTS + 1:])
    (x_ref, mod_ref, g_ref, w_ref, wt_ref, ones_ref, qg_ref, kg_ref,
     cw_ref, wgk_ref, bgk_ref, tri_ref, gg_ref,
     dk_ref, dz_ref, qt_ref, vt_ref, og_ref, xbuf, zbuf, state) = refs
    D = D_MODEL
    T = PROJ_TM

    @pl.when(pl.program_id(1) == 0)
    def _():
        xbuf[0:8, :] = jnp.zeros((8, GLA_CONV_WIDTH), F32)
        state[...] = jnp.zeros(state.shape, F32)

    x = x_ref[0]
    if fuse_prev:
        x = _gated_residual(x, ogp_ref, odp_ref, modp_ref, wout_ref)
        xo_ref[0] = x
    mod = mod_ref[0]
    shift = mod[:, 0:D]
    scale = mod[:, D:2 * D]
    ms = jnp.mean(x * x, axis=-1, keepdims=True)
    y = x * lax.rsqrt(ms + EPS) * g_ref[...]
    h = (y * (1.0 + scale) + shift).astype(BF16)

    def proj(a, n):
        return jnp.dot(h, w_ref[:, a:a + n], preferred_element_type=F32)

    def proj_t(a, n):
        return lax.dot_general(wt_ref[a:a + n, :], h, (((1,), (1,)), ((), ())),
                               preferred_element_type=F32)

    xbuf[8:8 + T, 0:512] = proj(0, 512)
    xbuf[8:8 + T, 512:1024] = proj(512, 512)
    gla_q, gla_k, gla_v = _gla_conv(xbuf, cw_ref)
    glr = proj(2560, GATE_PAD).astype(BF16)
    zbuf[...] = proj(1024, 512)
    dk_raw = proj(1536, 512)
    gate = jnp.dot(glr, wgk_ref[...], preferred_element_type=F32) + bgk_ref[...]
    dz_ref[0] = proj(2048, 512).astype(BF16)
    dk_ref[0] = _group_rms(dk_raw, ones_ref[...], kg_ref[...]).astype(BF16)

    qt = proj_t(0, DIFF_QK)
    bcum = _gla_log_decay(gate, tri_ref)
    vt_ref[0, 0] = proj_t(DIFF_QK, DIFF_WIDTH).astype(BF16)
    for g in range(DIFF_QK // DIFF_DH):
        r0 = g * DIFF_DH
        t = qt[r0:r0 + DIFF_DH, :]
        ss = jnp.sum(t * t, axis=0, keepdims=True)
        t = t * lax.rsqrt(ss * (1.0 / DIFF_DH) + EPS) * qg_ref[r0:r0 + DIFF_DH, :]
        qt_ref[0, 0, r0:r0 + DIFF_DH, :] = (t * (DIFF_DH ** -0.5 * LOG2E)).astype(BF16)

    _gla_tile(gla_q, gla_k, gla_v, zbuf, state, bcum, gg_ref, og_ref)


def _in_projection(x, prev, mod_l, norm_g, w_nat, w_t, ones_bd, qg_full, kg,
                   conv_w8, wgk_pad, bgk, tri, gla_g):
    B, S, D = x.shape
    tm = PROJ_TM
    NW = w_nat.shape[1]
    row = lambda b, i: (b, i, 0)
    tile = lambda b, i: (b, i, 0, 0)
    const = lambda b, i: (0, 0)
    per_batch = lambda b, i: (b, 0, 0)
    once = pl.Buffered(1)
    fuse_prev = prev is not None
    prev_specs = [pl.BlockSpec((1, tm, GLA_WIDTH), row),
                  pl.BlockSpec((1, tm, DIFF_WIDTH), row),
                  pl.BlockSpec((1, 1, 3 * D), per_batch),
                  pl.BlockSpec((MIX_WIDTH, D), const, pipeline_mode=once)] if fuse_prev else []
    x_out_shape = [jax.ShapeDtypeStruct((B, S, D), F32)] if fuse_prev else []
    x_out_spec = [pl.BlockSpec((1, tm, D), row)] if fuse_prev else []
    return pl.pallas_call(
        functools.partial(_proj_kernel, fuse_prev=fuse_prev),
        out_shape=x_out_shape + [
            jax.ShapeDtypeStruct((B, S, DIFF_QK), BF16),
            jax.ShapeDtypeStruct((B, S, DIFF_WIDTH), BF16),
            jax.ShapeDtypeStruct((B, S // tm, DIFF_QK, tm), BF16),
            jax.ShapeDtypeStruct((B, S // tm, DIFF_WIDTH, tm), BF16),
            jax.ShapeDtypeStruct((B, S, GLA_WIDTH), BF16)],
        grid=(B, S // tm),
        in_specs=[pl.BlockSpec((1, tm, D), row)] + prev_specs + [
            pl.BlockSpec((1, 1, 3 * D), per_batch),
            pl.BlockSpec((1, D), const),
            pl.BlockSpec((D, NW), const, pipeline_mode=once),
            pl.BlockSpec((DIFF_QK + DIFF_WIDTH, D), const, pipeline_mode=once),
            pl.BlockSpec((512, 512), const, pipeline_mode=once),
            pl.BlockSpec((DIFF_QK, tm), const, pipeline_mode=once),
            pl.BlockSpec((1, 512), const),
            pl.BlockSpec((8, GLA_CONV_WIDTH), const),
            pl.BlockSpec((GATE_PAD, GLA_QK), const),
            pl.BlockSpec((1, GLA_QK), const),
            pl.BlockSpec((tm, tm), const, pipeline_mode=once),
            pl.BlockSpec((1, GLA_DV), const),
        ],
        out_specs=x_out_spec + [
            pl.BlockSpec((1, tm, DIFF_QK), row),
            pl.BlockSpec((1, tm, DIFF_WIDTH), row),
            pl.BlockSpec((1, 1, DIFF_QK, tm), tile),
            pl.BlockSpec((1, 1, DIFF_WIDTH, tm), tile),
            pl.BlockSpec((1, tm, GLA_WIDTH), row)],
        scratch_shapes=[
            pltpu.VMEM((tm + 8, GLA_CONV_WIDTH), F32),
            pltpu.VMEM((tm, GLA_WIDTH), F32),
            pltpu.VMEM((GLA_HEADS, GLA_DV, LANES), F32),
        ],
        compiler_params=pltpu.CompilerParams(
            dimension_semantics=("arbitrary", "arbitrary"), vmem_limit_bytes=VMEM_LIMIT),
        name="out_in_proj_gla" if fuse_prev else "in_proj_gla",
    )(x, *(prev or ()), mod_l, norm_g, w_nat, w_t, ones_bd, qg_full, kg, conv_w8, wgk_pad,
      bgk, tri, gla_g)


def _split_maps(qt_ref, s):
    qt = qt_ref[0, s]
    row = lax.broadcasted_iota(jnp.int32, qt.shape, 0)
    zero = jnp.zeros_like(qt)
    return (jnp.where(row < DIFF_DH, qt, zero), jnp.where(row >= DIFF_DH, qt, zero))


def _chunk_mask():
    kc = lax.broadcasted_iota(jnp.int32, (ATT_T, ATT_T), 0) // CHUNK
    qc = lax.broadcasted_iota(jnp.int32, (ATT_T, ATT_T), 1) // CHUNK
    return kc <= qc


def _values_with_ones(vt_ref, j):
    return jnp.concatenate([vt_ref[0, j], jnp.ones((ATT_LROWS, ATT_T), BF16)], axis=0)


def _attn_finalize(s, acc0, l0, acc1, l1, z_ref, lq1_ref, lk1_ref, lq2_ref, lk2_ref, li_ref,
                   g_ref, o_ref):
    lam_init = li_ref[:, 0:1]
    lam = (jnp.exp(jnp.sum(lq1_ref[...] * lk1_ref[...], axis=-1, keepdims=True))
           - jnp.exp(jnp.sum(lq2_ref[...] * lk2_ref[...], axis=-1, keepdims=True))
           + lam_init)
    out = acc0 / l0 - lam * (acc1 / l1)
    ms = jnp.mean(out * out, axis=0, keepdims=True)
    out = out * lax.rsqrt(ms + EPS) * g_ref[...] * (1.0 - lam_init)
    start = s * ATT_T
    rows = pl.ds(start if isinstance(s, int) else pl.multiple_of(start, ATT_T), ATT_T)
    o_ref[0, rows, :] = (out.T * _silu(z_ref[0, rows, :].astype(F32))).astype(BF16)


def _attn_kernel_online(qt_ref, k_ref, vt_ref, z_ref, lq1_ref, lk1_ref, lq2_ref, lk2_ref,
                        li_ref, g_ref, o_ref, s_a, s_b, cm_a, cm_b, m_s, acc_s):
    T = ATT_T

    def tile(s, carry):
        i = pl.program_id(2) * ATT_QS + s
        qt_maps = _split_maps(qt_ref, s)

        m_s[...] = jnp.full(m_s.shape, -jnp.inf, F32)
        acc_s[...] = jnp.zeros(acc_s.shape, F32)

        def scores(j, s_ref, cm_ref):
            kj = k_ref[0, pl.ds(pl.multiple_of(j * T, T), T), :]
            for mp in range(2):
                s = jnp.dot(kj, qt_maps[mp], preferred_element_type=F32)
                s_ref[mp] = s
                cm_ref[mp] = jnp.max(s, axis=0, keepdims=True)

        def update(j, s_ref, cm_ref, mask):
            vtj = _values_with_ones(vt_ref, j)
            for mp in range(2):
                s = s_ref[mp]
                if mask is None:
                    tile_max = cm_ref[mp]
                else:
                    s = jnp.where(mask, s, -jnp.inf)
                    tile_max = jnp.max(s, axis=0, keepdims=True)
                m_old = m_s[mp]
                m_new = jnp.maximum(m_old, tile_max)
                alpha = jnp.exp2(m_old - m_new)
                p = jnp.exp2(s - m_new)
                acc_s[mp] = alpha * acc_s[mp] + jnp.dot(vtj, p.astype(BF16),
                                                        preferred_element_type=F32)
                m_s[mp] = m_new

        scores(0, s_a, cm_a)

        def pair(t, carry):
            j = 2 * t
            scores(j + 1, s_b, cm_b)
            update(j, s_a, cm_a, None)
            scores(j + 2, s_a, cm_a)
            update(j + 1, s_b, cm_b, None)
            return carry

        lax.fori_loop(0, i // 2, pair, 0)
        mask = _chunk_mask()

        @pl.when(i % 2 == 0)
        def _():
            update(i, s_a, cm_a, mask)

        @pl.when(i % 2 == 1)
        def _():
            scores(i, s_b, cm_b)
            update(i - 1, s_a, cm_a, None)
            update(i, s_b, cm_b, mask)

        acc0, acc1 = acc_s[0], acc_s[1]
        _attn_finalize(s, acc0[0:DIFF_DV], acc0[DIFF_DV:DIFF_DV + 1],
                       acc1[0:DIFF_DV], acc1[DIFF_DV:DIFF_DV + 1], z_ref,
                       lq1_ref, lk1_ref, lq2_ref, lk2_ref, li_ref, g_ref, o_ref)
        return carry

    lax.fori_loop(0, ATT_QS, tile, 0)


def _attn_kernel_bounded(qt_ref, k_ref, vt_ref, z_ref, lq1_ref, lk1_ref, lq2_ref, lk2_ref,
                         li_ref, g_ref, o_ref, acc_s, l_s):
    T, U = ATT_T, ATT_UNROLL

    def finalize(slot, s_out):
        _attn_finalize(s_out, acc_s[slot, 0], l_s[slot, 0], acc_s[slot, 1], l_s[slot, 1],
                       z_ref, lq1_ref, lk1_ref, lq2_ref, lk2_ref, li_ref, g_ref, o_ref)

    acc_s[1] = jnp.zeros(acc_s.shape[1:], F32)
    l_s[1] = jnp.ones(l_s.shape[1:], F32)

    def tile(s, carry):
        i = pl.program_id(2) * ATT_QS + s
        slot = s % 2
        qt_maps = _split_maps(qt_ref, s)

        def group(j0, n, mask):
            H = T // 2
            ps = ([], [])
            p_hi = [None, None]
            lsum = [None, None]
            for u in range(n):
                kj = k_ref[0, pl.ds(pl.multiple_of((j0 + u) * T, T), T), :]
                diag = mask is not None and u == n - 1
                for mp in range(2):
                    if diag:
                        p = jnp.exp2(jnp.dot(kj[0:H], qt_maps[mp], preferred_element_type=F32))
                        p = jnp.where(mask[0:H], p, 0.0)
                        ph = jnp.exp2(jnp.dot(kj[H:T], qt_maps[mp][:, H:T],
                                              preferred_element_type=F32))
                        ph = jnp.where(mask[H:T, H:T], ph, 0.0)
                        psum = jnp.sum(p, axis=0, keepdims=True) + jnp.concatenate(
                            [jnp.zeros((1, H), F32), jnp.sum(ph, axis=0, keepdims=True)], axis=1)
                        p_hi[mp] = ph.astype(BF16)
                    else:
                        p = jnp.exp2(jnp.dot(kj, qt_maps[mp], preferred_element_type=F32))
                        psum = jnp.sum(p, axis=0, keepdims=True)
                    lsum[mp] = psum if lsum[mp] is None else lsum[mp] + psum
                    ps[mp].append(p.astype(BF16))
            vts = [vt_ref[0, j0 + u] for u in range(n)]
            if mask is not None:
                vt_hi = vts[-1][:, H:T]
                vts[-1] = vts[-1][:, 0:H]
            vt = jnp.concatenate(vts, axis=1)
            for mp in range(2):
                p_cat = jnp.concatenate(ps[mp], axis=0)
                pv = jnp.dot(vt, p_cat, preferred_element_type=F32)
                if mask is not None:
                    pv_hi = jnp.dot(vt_hi, p_hi[mp], preferred_element_type=F32)
                    pv = pv + jnp.concatenate([jnp.zeros((DIFF_DV, H), F32), pv_hi], axis=1)
                    acc_s[slot, mp] = pv
                    l_s[slot, mp] = lsum[mp]
                else:
                    acc_s[slot, mp] += pv
                    l_s[slot, mp] += lsum[mp]

        mask = _chunk_mask()
        rem = i % U
        for r in range(U):
            @pl.when(rem == r)
            def _():
                finalize(1 - slot, jnp.maximum(s - 1, 0))
                group(i - r, r + 1, mask)

        def body(t, carry):
            group(t * U, U, None)
            return carry

        lax.fori_loop(0, i // U, body, 0)
        return carry

    lax.fori_loop(0, ATT_QS, tile, 0)
    finalize((ATT_QS - 1) % 2, ATT_QS - 1)


def _diff_attention(bounded, qt, dk, vt, dz, lq1, lk1, lq2, lk2, lam_init_row, g_full):
    B, S, _ = dk.shape
    T = ATT_T
    H = DIFF_HEADS
    nt = S // T
    QS = ATT_QS
    qmap = lambda b, h, i: (b, i, h)
    const = lambda b, h, i: (0, 0)
    if bounded:
        body = _attn_kernel_bounded
        scratch = [pltpu.VMEM((2, 2, DIFF_DV, T), F32), pltpu.VMEM((2, 2, 1, T), F32)]
    else:
        body = _attn_kernel_online
        scratch = [pltpu.VMEM((2, T, T), F32), pltpu.VMEM((2, T, T), F32),
                   pltpu.VMEM((2, 1, T), F32), pltpu.VMEM((2, 1, T), F32),
                   pltpu.VMEM((2, 1, T), F32),
                   pltpu.VMEM((2, DIFF_DV + ATT_LROWS, T), F32)]
    return pl.pallas_call(
        body,
        out_shape=jax.ShapeDtypeStruct((B, S, DIFF_WIDTH), BF16),
        grid=(B, H, nt // QS),
        in_specs=[
            pl.BlockSpec((1, QS, 2 * DIFF_DH, T), lambda b, h, i: (b, i, h, 0)),
            pl.BlockSpec((1, S, LANES), lambda b, h, i: (b, 0, h)),
            pl.BlockSpec((1, nt, DIFF_DV, T), lambda b, h, i: (b, 0, h, 0)),
            pl.BlockSpec((1, QS * T, LANES), qmap),
            pl.BlockSpec((1, DIFF_DH), const),
            pl.BlockSpec((1, DIFF_DH), const),
            pl.BlockSpec((1, DIFF_DH), const),
            pl.BlockSpec((1, DIFF_DH), const),
            pl.BlockSpec((1, LANES), const),
            pl.BlockSpec((DIFF_DV, T), const),
        ],
        out_specs=pl.BlockSpec((1, QS * T, LANES), qmap),
        scratch_shapes=scratch,
        compiler_params=pltpu.CompilerParams(
            dimension_semantics=("arbitrary", "arbitrary", "arbitrary"),
            vmem_limit_bytes=VMEM_LIMIT),
        name="diff_attn_bounded" if bounded else "diff_attn_online",
    )(qt, dk, vt, dz, lq1, lk1, lq2, lk2, lam_init_row, g_full)


def _out_kernel(x_ref, og_ref, od_ref, mod_ref, w_ref, o_ref):
    o_ref[0] = _gated_residual(x_ref[0], og_ref, od_ref, mod_ref, w_ref)


def _out_projection(x, o_gla, o_diff, mod_l, w_out):
    B, S, D = x.shape
    tm = OUT_TM
    row = lambda b, i: (b, i, 0)
    return pl.pallas_call(
        _out_kernel,
        out_shape=jax.ShapeDtypeStruct((B, S, D), F32),
        grid=(B, S // tm),
        in_specs=[
            pl.BlockSpec((1, tm, D), row),
            pl.BlockSpec((1, tm, GLA_WIDTH), row),
            pl.BlockSpec((1, tm, DIFF_WIDTH), row),
            pl.BlockSpec((1, 1, 3 * D), lambda b, i: (b, 0, 0)),
            pl.BlockSpec((MIX_WIDTH, D), lambda b, i: (0, 0), pipeline_mode=pl.Buffered(1)),
        ],
        out_specs=pl.BlockSpec((1, tm, D), row),
        compiler_params=pltpu.CompilerParams(
            dimension_semantics=("arbitrary", "arbitrary"), vmem_limit_bytes=VMEM_LIMIT),
        name="out_proj",
    )(x, o_gla, o_diff, mod_l, w_out)


def _split_w_in(w):
    idx = [0]
    for s in SPLIT_SIZES:
        idx.append(idx[-1] + s)
    gq, gk, gv, glr, gz, dq, dk, dv, dz = [w[:, idx[n]:idx[n + 1]] for n in range(9)]
    glr = jnp.pad(glr, ((0, 0), (0, GATE_PAD - GLA_GATE_RANK)))
    w_nat = jnp.concatenate([gq, gk, gv, gz, dk, dz, glr], axis=1).astype(BF16)
    w_t = jnp.concatenate([dq, dv], axis=1).T.astype(BF16)
    return w_nat, w_t


def kernel(x, c, w_ada, b_ada, norm_g, w_in, conv_w, w_gk, b_gk, gla_norm_g,
           qn_g, kn_g, lam_q1, lam_k1, lam_q2, lam_k2, diff_norm_g, w_out):
    B, S, D = x.shape
    mod = _modulation(c, w_ada, b_ada)

    r = jnp.arange(DIFF_QK) // DIFF_DH
    ones_bd = (r[:, None] == r[None, :]).astype(BF16)
    t = jnp.arange(PROJ_TM)
    tri = ((t[:, None] // CHUNK == t[None, :] // CHUNK)
           & (t[None, :] <= t[:, None])).astype(BF16)

    prev = None
    for l in range(DEPTH):
        mod_l = mod[l].reshape(B, 1, 3 * D)
        w_nat, w_t = _split_w_in(w_in[l])
        qg_full = jnp.broadcast_to(
            jnp.tile(qn_g[l], DIFF_QK // DIFF_DH)[:, None], (DIFF_QK, PROJ_TM))
        kg = jnp.tile(kn_g[l], DIFF_QK // DIFF_DH).reshape(1, DIFF_QK)
        conv_w8 = jnp.pad(conv_w[l], ((0, 8 - CONV_K), (0, 0)))
        wgk_pad = jnp.pad(w_gk[l], ((0, GATE_PAD - GLA_GATE_RANK), (0, 0))).astype(BF16)
        outs = _in_projection(
            x, prev, mod_l, norm_g[l].reshape(1, D), w_nat, w_t, ones_bd, qg_full, kg,
            conv_w8, wgk_pad, b_gk[l].reshape(1, GLA_QK), tri, gla_norm_g[l].reshape(1, GLA_DV))
        if prev is not None:
            x, outs = outs[0], outs[1:]
        dk, dz, qt, vt, o_gla = outs

        lam_init = 0.8 - 0.6 * math.exp(-0.3 * l)
        score_bound = (DIFF_DH ** 0.5 * LOG2E * 1.01
                       * jnp.max(jnp.abs(qn_g[l])) * jnp.max(jnp.abs(kn_g[l])))
        attn_args = (
            qt, dk, vt, dz,
            lam_q1[l].reshape(1, DIFF_DH), lam_k1[l].reshape(1, DIFF_DH),
            lam_q2[l].reshape(1, DIFF_DH), lam_k2[l].reshape(1, DIFF_DH),
            jnp.full((1, LANES), lam_init, F32),
            jnp.broadcast_to(diff_norm_g[l][:, None], (DIFF_DV, ATT_T)))
        o_diff = lax.cond(score_bound <= ATT_SCORE_BOUND,
                          functools.partial(_diff_attention, True),
                          functools.partial(_diff_attention, False), *attn_args)

        prev = (o_gla, o_diff, mod_l, w_out[l].astype(BF16))
    return _out_projection(x, *prev)
```

```python
import functools
import math

import jax
import jax.numpy as jnp
from jax import lax
from jax.experimental import pallas as pl
from jax.experimental.pallas import tpu as pltpu

D_MODEL = 1024
DEPTH = 4
CHUNK = 64
CONV_K = 4
EPS = 1e-6

GLA_HEADS = 4
GLA_DK = 64
GLA_DV = 128
GLA_QK = GLA_HEADS * GLA_DK
GLA_WIDTH = GLA_HEADS * GLA_DV
GLA_GATE_RANK = 16
GLA_GATE_TEMP = 16.0

DIFF_HEADS = 4
DIFF_DH = 64
DIFF_DV = 2 * DIFF_DH
DIFF_QK = DIFF_HEADS * 2 * DIFF_DH
DIFF_WIDTH = DIFF_HEADS * DIFF_DV

MIX_WIDTH = GLA_WIDTH + DIFF_WIDTH
GLA_CONV_WIDTH = 2 * GLA_QK + GLA_WIDTH
SPLIT_SIZES = (GLA_QK, GLA_QK, GLA_WIDTH, GLA_GATE_RANK, GLA_WIDTH,
               DIFF_QK, DIFF_QK, DIFF_WIDTH, DIFF_WIDTH)

LANES = 128
GATE_PAD = LANES
VMEM_LIMIT = 56 * 1024 * 1024

ATT_T = 512
ATT_QS = 8
ATT_UNROLL = 8
ATT_SCORE_BOUND = 32.0
ATT_LROWS = 16
PROJ_TM = ATT_T
OUT_TM = 512
MOD_TN = 512

F32 = jnp.float32
BF16 = jnp.bfloat16
LOG2E = math.log2(math.e)


def _sigmoid(x):
    return 1.0 / (1.0 + jnp.exp(-x))


def _silu(x):
    return x * _sigmoid(x)


def _mod_kernel(ct_ref, w_ref, b_ref, o_ref):
    ct = ct_ref[...]
    ca = _silu(ct)
    w = w_ref[0]
    rows = []
    for b in range(ct.shape[1]):
        rows.append(jnp.sum(w * ca[:, b:b + 1], axis=0, keepdims=True))
    o_ref[0] = jnp.concatenate(rows, axis=0) + b_ref[0]


def _modulation(c, w_ada, b_ada):
    B, D = c.shape
    L, _, N = w_ada.shape
    return pl.pallas_call(
        _mod_kernel,
        out_shape=jax.ShapeDtypeStruct((L, B, N), F32),
        grid=(L, N // MOD_TN),
        in_specs=[
            pl.BlockSpec((D, B), lambda l, n: (0, 0)),
            pl.BlockSpec((1, D, MOD_TN), lambda l, n: (l, 0, n)),
            pl.BlockSpec((1, 1, MOD_TN), lambda l, n: (l, 0, n)),
        ],
        out_specs=pl.BlockSpec((1, B, MOD_TN), lambda l, n: (l, 0, n)),
        compiler_params=pltpu.CompilerParams(
            dimension_semantics=("arbitrary", "arbitrary"), vmem_limit_bytes=VMEM_LIMIT),
        name="adaln_mod",
    )(c.T, w_ada, b_ada.reshape(L, 1, N))


def _gated_residual(x, og_ref, od_ref, mod_ref, w_ref):
    gate = mod_ref[0][:, 2 * D_MODEL:3 * D_MODEL]
    y = (jnp.dot(og_ref[0], w_ref[0:GLA_WIDTH, :], preferred_element_type=F32)
         + jnp.dot(od_ref[0], w_ref[GLA_WIDTH:MIX_WIDTH, :], preferred_element_type=F32))
    return x + gate * y


def _group_rms(t, ones_bd, gain):
    ss = jnp.dot((t * t).astype(BF16), ones_bd, preferred_element_type=F32)
    return t * lax.rsqrt(ss * (1.0 / DIFF_DH) + EPS) * gain


def _gla_log_decay(gate, tri_ref):
    log_a = -(jnp.maximum(-gate, 0.0) + jnp.log1p(jnp.exp(-jnp.abs(gate))))
    log_a = log_a * (1.0 / GLA_GATE_TEMP)
    la_hi = log_a.astype(BF16)
    la_lo = (log_a - la_hi.astype(F32)).astype(BF16)
    tri = tri_ref[...]
    return (jnp.dot(tri, la_hi, preferred_element_type=F32)
            + jnp.dot(tri, la_lo, preferred_element_type=F32))


def _gla_conv(xbuf, cw_ref):
    T = PROJ_TM
    cw = cw_ref[...]
    conv = cw[0:1, :] * xbuf[pl.ds(8 - (CONV_K - 1), T), :]
    for j in range(1, CONV_K):
        conv = conv + cw[j:j + 1, :] * xbuf[pl.ds(8 - (CONV_K - 1) + j, T), :]
    xbuf[0:8, :] = xbuf[T:T + 8, :]
    act = _silu(conv)
    q = (act[:, 0:GLA_QK] * (GLA_DK ** -0.5)).astype(BF16)
    k = act[:, GLA_QK:2 * GLA_QK]
    v = act[:, 2 * GLA_QK:].astype(BF16)
    return q, k, v


def _gla_tile(q, k, v, zbuf, state, bcum, g_ref, o_ref):
    T = PROJ_TM
    n_chunks = T // CHUNK

    lane = lax.broadcasted_iota(jnp.int32, (CHUNK, LANES), 1)
    head_lanes = [(lane >= hh * GLA_DK) & (lane < (hh + 1) * GLA_DK) for hh in range(2)]

    decay, outer = [], []
    for c in range(n_chunks):
        r0 = c * CHUNK
        b_c = bcum[r0:r0 + CHUNK, :]
        b_end = b_c[CHUNK - 1:CHUNK, :]
        kdec = k[r0:r0 + CHUNK, :] * jnp.exp(b_end - b_c)
        decay.append(jnp.exp(b_end))
        for h in range(GLA_HEADS):
            p = h // 2
            kd_m = jnp.where(head_lanes[h % 2], kdec[:, p * LANES:(p + 1) * LANES], 0.0)
            v_h = v[r0:r0 + CHUNK, h * GLA_DV:(h + 1) * GLA_DV]
            outer.append(lax.dot_general(v_h, kd_m.astype(BF16), (((0,), (0,)), ((), ())),
                                         preferred_element_type=F32))

    g_out = g_ref[...]
    for c in range(n_chunks):
        r0 = c * CHUNK
        for h in range(GLA_HEADS):
            p = h // 2
            s_new = state[h] * decay[c][:, p * LANES:(p + 1) * LANES] + outer[c * GLA_HEADS + h]
            state[h] = s_new
            q_pair = q[r0:r0 + CHUNK, p * LANES:(p + 1) * LANES]
            o_h = lax.dot_general(q_pair, s_new.astype(BF16), (((1,), (1,)), ((), ())),
                                  preferred_element_type=F32)
            ms = jnp.mean(o_h * o_h, axis=-1, keepdims=True)
            zz = zbuf[r0:r0 + CHUNK, h * GLA_DV:(h + 1) * GLA_DV]
            res = o_h * lax.rsqrt(ms + EPS) * g_out * _silu(zz)
            o_ref[0, r0:r0 + CHUNK, h * GLA_DV:(h + 1) * GLA_DV] = res.astype(BF16)


N_PROJ_INPUTS = 13
N_PREV_INPUTS = 4


def _proj_kernel(*refs, fuse_prev):
    if fuse_prev:
        ogp_ref, odp_ref, modp_ref, wout_ref = refs[1:1 + N_PREV_INPUTS]
        xo_ref = refs[N_PROJ_INPUTS + N_PREV_INPUTS]
        refs = (refs[:1] + refs[1 + N_PREV_INPUTS:N_PROJ_INPUTS + N_PREV_INPUTS]
                + refs[N_PROJ_INPUTS + N_PREV_INPUTS + 1:])
    (x_ref, mod_ref, g_ref, w_ref, wt_ref, ones_ref, qg_ref, kg_ref,
     cw_ref, wgk_ref, bgk_ref, tri_ref, gg_ref,
     dk_ref, dz_ref, qt_ref, vt_ref, og_ref, xbuf, zbuf, state) = refs
    D = D_MODEL
    T = PROJ_TM

    @pl.when(pl.program_id(1) == 0)
    def _():
        xbuf[0:8, :] = jnp.zeros((8, GLA_CONV_WIDTH), F32)
        state[...] = jnp.zeros(state.shape, F32)

    x = x_ref[0]
    if fuse_prev:
        x = _gated_residual(x, ogp_ref, odp_ref, modp_ref, wout_ref)
        xo_ref[0] = x
    mod = mod_ref[0]
    shift = mod[:, 0:D]
    scale = mod[:, D:2 * D]
    ms = jnp.mean(x * x, axis=-1, keepdims=True)
    y = x * lax.rsqrt(ms + EPS) * g_ref[...]
    h = (y * (1.0 + scale) + shift).astype(BF16)

    def proj(a, n):
        return jnp.dot(h, w_ref[:, a:a + n], preferred_element_type=F32)

    def proj_t(a, n):
        return lax.dot_general(wt_ref[a:a + n, :], h, (((1,), (1,)), ((), ())),
                               preferred_element_type=F32)

    xbuf[8:8 + T, 0:512] = proj(0, 512)
    xbuf[8:8 + T, 512:1024] = proj(512, 512)
    gla_q, gla_k, gla_v = _gla_conv(xbuf, cw_ref)
    glr = proj(2560, GATE_PAD).astype(BF16)
    zbuf[...] = proj(1024, 512)
    dk_raw = proj(1536, 512)
    gate = jnp.dot(glr, wgk_ref[...], preferred_element_type=F32) + bgk_ref[...]
    dz_ref[0] = proj(2048, 512).astype(BF16)
    dk_ref[0] = _group_rms(dk_raw, ones_ref[...], kg_ref[...]).astype(BF16)

    qt = proj_t(0, DIFF_QK)
    bcum = _gla_log_decay(gate, tri_ref)
    vt_ref[0, 0] = proj_t(DIFF_QK, DIFF_WIDTH).astype(BF16)
    for g in range(DIFF_QK // DIFF_DH):
        r0 = g * DIFF_DH
        t = qt[r0:r0 + DIFF_DH, :]
        ss = jnp.sum(t * t, axis=0, keepdims=True)
        t = t * lax.rsqrt(ss * (1.0 / DIFF_DH) + EPS) * qg_ref[r0:r0 + DIFF_DH, :]
        qt_ref[0, 0, r0:r0 + DIFF_DH, :] = (t * (DIFF_DH ** -0.5 * LOG2E)).astype(BF16)

    _gla_tile(gla_q, gla_k, gla_v, zbuf, state, bcum, gg_ref, og_ref)


def _in_projection(x, prev, mod_l, norm_g, w_nat, w_t, ones_bd, qg_full, kg,
                   conv_w8, wgk_pad, bgk, tri, gla_g):
    B, S, D = x.shape
    tm = PROJ_TM
    NW = w_nat.shape[1]
    row = lambda b, i: (b, i, 0)
    tile = lambda b, i: (b, i, 0, 0)
    const = lambda b, i: (0, 0)
    per_batch = lambda b, i: (b, 0, 0)
    once = pl.Buffered(1)
    fuse_prev = prev is not None
    prev_specs = [pl.BlockSpec((1, tm, GLA_WIDTH), row),
                  pl.BlockSpec((1, tm, DIFF_WIDTH), row),
                  pl.BlockSpec((1, 1, 3 * D), per_batch),
                  pl.BlockSpec((MIX_WIDTH, D), const, pipeline_mode=once)] if fuse_prev else []
    x_out_shape = [jax.ShapeDtypeStruct((B, S, D), F32)] if fuse_prev else []
    x_out_spec = [pl.BlockSpec((1, tm, D), row)] if fuse_prev else []
    return pl.pallas_call(
        functools.partial(_proj_kernel, fuse_prev=fuse_prev),
        out_shape=x_out_shape + [
            jax.ShapeDtypeStruct((B, S, DIFF_QK), BF16),
            jax.ShapeDtypeStruct((B, S, DIFF_WIDTH), BF16),
            jax.ShapeDtypeStruct((B, S // tm, DIFF_QK, tm), BF16),
            jax.ShapeDtypeStruct((B, S // tm, DIFF_WIDTH, tm), BF16),
            jax.ShapeDtypeStruct((B, S, GLA_WIDTH), BF16)],
        grid=(B, S // tm),
        in_specs=[pl.BlockSpec((1, tm, D), row)] + prev_specs + [
            pl.BlockSpec((1, 1, 3 * D), per_batch),
            pl.BlockSpec((1, D), const),
            pl.BlockSpec((D, NW), const, pipeline_mode=once),
            pl.BlockSpec((DIFF_QK + DIFF_WIDTH, D), const, pipeline_mode=once),
            pl.BlockSpec((512, 512), const, pipeline_mode=once),
            pl.BlockSpec((DIFF_QK, tm), const, pipeline_mode=once),
            pl.BlockSpec((1, 512), const),
            pl.BlockSpec((8, GLA_CONV_WIDTH), const),
            pl.BlockSpec((GATE_PAD, GLA_QK), const),
            pl.BlockSpec((1, GLA_QK), const),
            pl.BlockSpec((tm, tm), const, pipeline_mode=once),
            pl.BlockSpec((1, GLA_DV), const),
        ],
        out_specs=x_out_spec + [
            pl.BlockSpec((1, tm, DIFF_QK), row),
            pl.BlockSpec((1, tm, DIFF_WIDTH), row),
            pl.BlockSpec((1, 1, DIFF_QK, tm), tile),
            pl.BlockSpec((1, 1, DIFF_WIDTH, tm), tile),
            pl.BlockSpec((1, tm, GLA_WIDTH), row)],
        scratch_shapes=[
            pltpu.VMEM((tm + 8, GLA_CONV_WIDTH), F32),
            pltpu.VMEM((tm, GLA_WIDTH), F32),
            pltpu.VMEM((GLA_HEADS, GLA_DV, LANES), F32),
        ],
        compiler_params=pltpu.CompilerParams(
            dimension_semantics=("arbitrary", "arbitrary"), vmem_limit_bytes=VMEM_LIMIT),
        name="out_in_proj_gla" if fuse_prev else "in_proj_gla",
    )(x, *(prev or ()), mod_l, norm_g, w_nat, w_t, ones_bd, qg_full, kg, conv_w8, wgk_pad,
      bgk, tri, gla_g)


def _split_maps(qt_ref, s):
    qt = qt_ref[0, s]
    row = lax.broadcasted_iota(jnp.int32, qt.shape, 0)
    zero = jnp.zeros_like(qt)
    return (jnp.where(row < DIFF_DH, qt, zero), jnp.where(row >= DIFF_DH, qt, zero))


def _chunk_mask():
    kc = lax.broadcasted_iota(jnp.int32, (ATT_T, ATT_T), 0) // CHUNK
    qc = lax.broadcasted_iota(jnp.int32, (ATT_T, ATT_T), 1) // CHUNK
    return kc <= qc


def _values_with_ones(vt_ref, j):
    return jnp.concatenate([vt_ref[0, j], jnp.ones((ATT_LROWS, ATT_T), BF16)], axis=0)


def _attn_finalize(s, acc0, l0, acc1, l1, z_ref, lq1_ref, lk1_ref, lq2_ref, lk2_ref, li_ref,
                   g_ref, o_ref):
    lam_init = li_ref[:, 0:1]
    lam = (jnp.exp(jnp.sum(lq1_ref[...] * lk1_ref[...], axis=-1, keepdims=True))
           - jnp.exp(jnp.sum(lq2_ref[...] * lk2_ref[...], axis=-1, keepdims=True))
           + lam_init)
    out = acc0 / l0 - lam * (acc1 / l1)
    ms = jnp.mean(out * out, axis=0, keepdims=True)
    out = out * lax.rsqrt(ms + EPS) * g_ref[...] * (1.0 - lam_init)
    start = s * ATT_T
    rows = pl.ds(start if isinstance(s, int) else pl.multiple_of(start, ATT_T), ATT_T)
    o_ref[0, rows, :] = (out.T * _silu(z_ref[0, rows, :].astype(F32))).astype(BF16)


def _attn_kernel_online(qt_ref, k_ref, vt_ref, z_ref, lq1_ref, lk1_ref, lq2_ref, lk2_ref,
                        li_ref, g_ref, o_ref, s_a, s_b, cm_a, cm_b, m_s, acc_s):
    T = ATT_T

    def tile(s, carry):
        i = pl.program_id(2) * ATT_QS + s
        qt_maps = _split_maps(qt_ref, s)

        m_s[...] = jnp.full(m_s.shape, -jnp.inf, F32)
        acc_s[...] = jnp.zeros(acc_s.shape, F32)

        def scores(j, s_ref, cm_ref):
            kj = k_ref[0, pl.ds(pl.multiple_of(j * T, T), T), :]
            for mp in range(2):
                s = jnp.dot(kj, qt_maps[mp], preferred_element_type=F32)
                s_ref[mp] = s
                cm_ref[mp] = jnp.max(s, axis=0, keepdims=True)

        def update(j, s_ref, cm_ref, mask):
            vtj = _values_with_ones(vt_ref, j)
            for mp in range(2):
                s = s_ref[mp]
                if mask is None:
                    tile_max = cm_ref[mp]
                else:
                    s = jnp.where(mask, s, -jnp.inf)
                    tile_max = jnp.max(s, axis=0, keepdims=True)
                m_old = m_s[mp]
                m_new = jnp.maximum(m_old, tile_max)
                alpha = jnp.exp2(m_old - m_new)
                p = jnp.exp2(s - m_new)
                acc_s[mp] = alpha * acc_s[mp] + jnp.dot(vtj, p.astype(BF16),
                                                        preferred_element_type=F32)
                m_s[mp] = m_new

        scores(0, s_a, cm_a)

        def pair(t, carry):
            j = 2 * t
            scores(j + 1, s_b, cm_b)
            update(j, s_a, cm_a, None)
            scores(j + 2, s_a, cm_a)
            update(j + 1, s_b, cm_b, None)
            return carry

        lax.fori_loop(0, i // 2, pair, 0)
        mask = _chunk_mask()

        @pl.when(i % 2 == 0)
        def _():
            update(i, s_a, cm_a, mask)

        @pl.when(i % 2 == 1)
        def _():
            scores(i, s_b, cm_b)
            update(i - 1, s_a, cm_a, None)
            update(i, s_b, cm_b, mask)

        acc0, acc1 = acc_s[0], acc_s[1]
        _attn_finalize(s, acc0[0:DIFF_DV], acc0[DIFF_DV:DIFF_DV + 1],
                       acc1[0:DIFF_DV], acc1[DIFF_DV:DIFF_DV + 1], z_ref,
                       lq1_ref, lk1_ref, lq2_ref, lk2_ref, li_ref, g_ref, o_ref)
        return carry

    lax.fori_loop(0, ATT_QS, tile, 0)


def _attn_kernel_bounded(qt_ref, k_ref, vt_ref, z_ref, lq1_ref, lk1_ref, lq2_ref, lk2_ref,
                         li_ref, g_ref, o_ref, acc_s, l_s):
    T, U = ATT_T, ATT_UNROLL

    def finalize(slot, s_out):
        _attn_finalize(s_out, acc_s[slot, 0], l_s[slot, 0], acc_s[slot, 1], l_s[slot, 1],
                       z_ref, lq1_ref, lk1_ref, lq2_ref, lk2_ref, li_ref, g_ref, o_ref)

    acc_s[1] = jnp.zeros(acc_s.shape[1:], F32)
    l_s[1] = jnp.ones(l_s.shape[1:], F32)

    def tile(s, carry):
        i = pl.program_id(2) * ATT_QS + s
        slot = s % 2
        qt_maps = _split_maps(qt_ref, s)

        def group(j0, n, mask):
            H = T // 2
            ps = ([], [])
            p_hi = [None, None]
            lsum = [None, None]
            for u in range(n):
                kj = k_ref[0, pl.ds(pl.multiple_of((j0 + u) * T, T), T), :]
                diag = mask is not None and u == n - 1
                for mp in range(2):
                    if diag:
                        p = jnp.exp2(jnp.dot(kj[0:H], qt_maps[mp], preferred_element_type=F32))
                        p = jnp.where(mask[0:H], p, 0.0)
                        ph = jnp.exp2(jnp.dot(kj[H:T], qt_maps[mp][:, H:T],
                                              preferred_element_type=F32))
                        ph = jnp.where(mask[H:T, H:T], ph, 0.0)
                        psum = jnp.sum(p, axis=0, keepdims=True) + jnp.concatenate(
                            [jnp.zeros((1, H), F32), jnp.sum(ph, axis=0, keepdims=True)], axis=1)
                        p_hi[mp] = ph.astype(BF16)
                    else:
                        p = jnp.exp2(jnp.dot(kj, qt_maps[mp], preferred_element_type=F32))
                        psum = jnp.sum(p, axis=0, keepdims=True)
                    lsum[mp] = psum if lsum[mp] is None else lsum[mp] + psum
                    ps[mp].append(p.astype(BF16))
            vts = [vt_ref[0, j0 + u] for u in range(n)]
            if mask is not None:
                vt_hi = vts[-1][:, H:T]
                vts[-1] = vts[-1][:, 0:H]
            vt = jnp.concatenate(vts, axis=1)
            for mp in range(2):
                p_cat = jnp.concatenate(ps[mp], axis=0)
                pv = jnp.dot(vt, p_cat, preferred_element_type=F32)
                if mask is not None:
                    pv_hi = jnp.dot(vt_hi, p_hi[mp], preferred_element_type=F32)
                    pv = pv + jnp.concatenate([jnp.zeros((DIFF_DV, H), F32), pv_hi], axis=1)
                    acc_s[slot, mp] = pv
                    l_s[slot, mp] = lsum[mp]
                else:
                    acc_s[slot, mp] += pv
                    l_s[slot, mp] += lsum[mp]

        mask = _chunk_mask()
        rem = i % U
        for r in range(U):
            @pl.when(rem == r)
            def _():
                finalize(1 - slot, jnp.maximum(s - 1, 0))
                group(i - r, r + 1, mask)

        def body(t, carry):
            group(t * 2 * U, 2 * U, None)
            return carry

        n_double = i // (2 * U)
        lax.fori_loop(0, n_double, body, 0)

        @pl.when((i // U) % 2 == 1)
        def _():
            group(n_double * 2 * U, U, None)

        return carry

    lax.fori_loop(0, ATT_QS, tile, 0)
    finalize((ATT_QS - 1) % 2, ATT_QS - 1)


def _diff_attention(bounded, qt, dk, vt, dz, lq1, lk1, lq2, lk2, lam_init_row, g_full):
    B, S, _ = dk.shape
    T = ATT_T
    H = DIFF_HEADS
    nt = S // T
    QS = ATT_QS
    qmap = lambda b, h, i: (b, i, h)
    const = lambda b, h, i: (0, 0)
    if bounded:
        body = _attn_kernel_bounded
        scratch = [pltpu.VMEM((2, 2, DIFF_DV, T), F32), pltpu.VMEM((2, 2, 1, T), F32)]
    else:
        body = _attn_kernel_online
        scratch = [pltpu.VMEM((2, T, T), F32), pltpu.VMEM((2, T, T), F32),
                   pltpu.VMEM((2, 1, T), F32), pltpu.VMEM((2, 1, T), F32),
                   pltpu.VMEM((2, 1, T), F32),
                   pltpu.VMEM((2, DIFF_DV + ATT_LROWS, T), F32)]
    return pl.pallas_call(
        body,
        out_shape=jax.ShapeDtypeStruct((B, S, DIFF_WIDTH), BF16),
        grid=(B, H, nt // QS),
        in_specs=[
            pl.BlockSpec((1, QS, 2 * DIFF_DH, T), lambda b, h, i: (b, i, h, 0)),
            pl.BlockSpec((1, S, LANES), lambda b, h, i: (b, 0, h)),
            pl.BlockSpec((1, nt, DIFF_DV, T), lambda b, h, i: (b, 0, h, 0)),
            pl.BlockSpec((1, QS * T, LANES), qmap),
            pl.BlockSpec((1, DIFF_DH), const),
            pl.BlockSpec((1, DIFF_DH), const),
            pl.BlockSpec((1, DIFF_DH), const),
            pl.BlockSpec((1, DIFF_DH), const),
            pl.BlockSpec((1, LANES), const),
            pl.BlockSpec((DIFF_DV, T), const),
        ],
        out_specs=pl.BlockSpec((1, QS * T, LANES), qmap),
        scratch_shapes=scratch,
        compiler_params=pltpu.CompilerParams(
            dimension_semantics=("arbitrary", "arbitrary", "arbitrary"),
            vmem_limit_bytes=VMEM_LIMIT),
        name="diff_attn_bounded" if bounded else "diff_attn_online",
    )(qt, dk, vt, dz, lq1, lk1, lq2, lk2, lam_init_row, g_full)


def _out_kernel(x_ref, og_ref, od_ref, mod_ref, w_ref, o_ref):
    o_ref[0] = _gated_residual(x_ref[0], og_ref, od_ref, mod_ref, w_ref)


def _out_projection(x, o_gla, o_diff, mod_l, w_out):
    B, S, D = x.shape
    tm = OUT_TM
    row = lambda b, i: (b, i, 0)
    return pl.pallas_call(
        _out_kernel,
        out_shape=jax.ShapeDtypeStruct((B, S, D), F32),
        grid=(B, S // tm),
        in_specs=[
            pl.BlockSpec((1, tm, D), row),
            pl.BlockSpec((1, tm, GLA_WIDTH), row),
            pl.BlockSpec((1, tm, DIFF_WIDTH), row),
            pl.BlockSpec((1, 1, 3 * D), lambda b, i: (b, 0, 0)),
            pl.BlockSpec((MIX_WIDTH, D), lambda b, i: (0, 0), pipeline_mode=pl.Buffered(1)),
        ],
        out_specs=pl.BlockSpec((1, tm, D), row),
        compiler_params=pltpu.CompilerParams(
            dimension_semantics=("arbitrary", "arbitrary"), vmem_limit_bytes=VMEM_LIMIT),
        name="out_proj",
    )(x, o_gla, o_diff, mod_l, w_out)


def _split_w_in(w):
    idx = [0]
    for s in SPLIT_SIZES:
        idx.append(idx[-1] + s)
    gq, gk, gv, glr, gz, dq, dk, dv, dz = [w[:, idx[n]:idx[n + 1]] for n in range(9)]
    glr = jnp.pad(glr, ((0, 0), (0, GATE_PAD - GLA_GATE_RANK)))
    w_nat = jnp.concatenate([gq, gk, gv, gz, dk, dz, glr], axis=1).astype(BF16)
    w_t = jnp.concatenate([dq, dv], axis=1).T.astype(BF16)
    return w_nat, w_t


def kernel(x, c, w_ada, b_ada, norm_g, w_in, conv_w, w_gk, b_gk, gla_norm_g,
           qn_g, kn_g, lam_q1, lam_k1, lam_q2, lam_k2, diff_norm_g, w_out):
    B, S, D = x.shape
    mod = _modulation(c, w_ada, b_ada)

    r = jnp.arange(DIFF_QK) // DIFF_DH
    ones_bd = (r[:, None] == r[None, :]).astype(BF16)
    t = jnp.arange(PROJ_TM)
    tri = ((t[:, None] // CHUNK == t[None, :] // CHUNK)
           & (t[None, :] <= t[:, None])).astype(BF16)

    prev = None
    for l in range(DEPTH):
        mod_l = mod[l].reshape(B, 1, 3 * D)
        w_nat, w_t = _split_w_in(w_in[l])
        qg_full = jnp.broadcast_to(
            jnp.tile(qn_g[l], DIFF_QK // DIFF_DH)[:, None], (DIFF_QK, PROJ_TM))
        kg = jnp.tile(kn_g[l], DIFF_QK // DIFF_DH).reshape(1, DIFF_QK)
        conv_w8 = jnp.pad(conv_w[l], ((0, 8 - CONV_K), (0, 0)))
        wgk_pad = jnp.pad(w_gk[l], ((0, GATE_PAD - GLA_GATE_RANK), (0, 0))).astype(BF16)
        outs = _in_projection(
            x, prev, mod_l, norm_g[l].reshape(1, D), w_nat, w_t, ones_bd, qg_full, kg,
            conv_w8, wgk_pad, b_gk[l].reshape(1, GLA_QK), tri, gla_norm_g[l].reshape(1, GLA_DV))
        if prev is not None:
            x, outs = outs[0], outs[1:]
        dk, dz, qt, vt, o_gla = outs

        lam_init = 0.8 - 0.6 * math.exp(-0.3 * l)
        score_bound = (DIFF_DH ** 0.5 * LOG2E * 1.01
                       * jnp.max(jnp.abs(qn_g[l])) * jnp.max(jnp.abs(kn_g[l])))
        attn_args = (
            qt, dk, vt, dz,
            lam_q1[l].reshape(1, DIFF_DH), lam_k1[l].reshape(1, DIFF_DH),
            lam_q2[l].reshape(1, DIFF_DH), lam_k2[l].reshape(1, DIFF_DH),
            jnp.full((1, LANES), lam_init, F32),
            jnp.broadcast_to(diff_norm_g[l][:, None], (DIFF_DV, ATT_T)))
        o_diff = lax.cond(score_bound <= ATT_SCORE_BOUND,
                          functools.partial(_diff_attention, True),
                          functools.partial(_diff_attention, False), *attn_args)

        prev = (o_gla, o_diff, mod_l, w_out[l].astype(BF16))
    return _out_projection(x, *prev)
```
